```python
import jax, jax.numpy as jnp
from jax import lax
import numpy as np

D_MODEL = 2048
BATCH = 16
SEQ = 2048
DEPTH = 2

D_MIX = D_MODEL
GROUP_DIM = 128
N_GROUPS = D_MIX // GROUP_DIM
MLA_WIDTH = D_MIX // 2
V_HEAD_DIM = 128
MLA_HEADS = MLA_WIDTH // V_HEAD_DIM
QK_NOPE_DIM = 128
QK_ROPE_DIM = 64
Q_LORA_RANK = 512
KV_LORA_RANK = 512
ROPE_THETA = 10000.0
Q_BLOCK = 128
CONF_WIDTH = D_MIX // 4
CONF_KERNEL = 31
SC_WIDTH = D_MIX - MLA_WIDTH - CONF_WIDTH
SC_KERNEL = 3
D_FF = 256 * ((8 * D_MODEL // 3 + 255) // 256)
IN_WIDTH = Q_LORA_RANK + KV_LORA_RANK + QK_ROPE_DIM + 2 * CONF_WIDTH + 3 * SC_WIDTH
RMS_EPS = 1e-6
LN_EPS = 1e-5

kernel_name = "hymba_mla_conformer_shortconv_macaron"


def rms_norm(x, g):
    xf = x.astype(jnp.float32)
    y = xf * lax.rsqrt(jnp.mean(xf * xf, axis=-1, keepdims=True) + RMS_EPS)
    return (y * g.astype(jnp.float32)).astype(x.dtype)


def layer_norm(x, g, b):
    xf = x.astype(jnp.float32)
    mu = jnp.mean(xf, axis=-1, keepdims=True)
    d = xf - mu
    var = jnp.mean(d * d, axis=-1, keepdims=True)
    return (d * lax.rsqrt(var + LN_EPS) * g.astype(jnp.float32) + b.astype(jnp.float32)).astype(x.dtype)


def swiglu(x, w_gate, w_up, w_down):
    return (jax.nn.silu(x @ w_gate) * (x @ w_up)) @ w_down


def causal_depthwise_conv(x, w):
    k, c = w.shape
    return lax.conv_general_dilated(
        x, w[:, None, :], window_strides=(1,), padding=[(k - 1, 0)],
        dimension_numbers=("NWC", "WIO", "NWC"), feature_group_count=c)


def rope_tables(seq):
    inv = 1.0 / (ROPE_THETA ** (jnp.arange(0, QK_ROPE_DIM, 2, dtype=jnp.float32) / QK_ROPE_DIM))
    ang = jnp.arange(seq, dtype=jnp.float32)[:, None] * inv[None, :]
    return jnp.cos(ang), jnp.sin(ang)


def apply_rope(x, cos, sin):
    xf = x.astype(jnp.float32)
    half = xf.shape[-1] // 2
    x1, x2 = xf[..., :half], xf[..., half:]
    return jnp.concatenate([x1 * cos - x2 * sin, x2 * cos + x1 * sin], axis=-1).astype(x.dtype)


def mla(c_q, c_kv, k_rope, q_norm, w_uq, kv_norm, w_ukv, cos, sin):
    b, s, _ = c_q.shape
    q = (rms_norm(c_q, q_norm) @ w_uq).reshape(b, s, MLA_HEADS, QK_NOPE_DIM + QK_ROPE_DIM)
    q_nope = q[..., :QK_NOPE_DIM]
    q_rope = apply_rope(q[..., QK_NOPE_DIM:], cos[:, None, :], sin[:, None, :])
    k_rope = apply_rope(k_rope, cos, sin)
    kv = (rms_norm(c_kv, kv_norm) @ w_ukv).reshape(b, s, MLA_HEADS, QK_NOPE_DIM + V_HEAD_DIM)
    k_nope, v = kv[..., :QK_NOPE_DIM], kv[..., QK_NOPE_DIM:]
    scale = (QK_NOPE_DIM + QK_ROPE_DIM) ** -0.5
    outs = []
    for i in range(s // Q_BLOCK):
        q0, q1 = i * Q_BLOCK, (i + 1) * Q_BLOCK
        sc = (jnp.einsum("bqhd,bkhd->bhqk", q_nope[:, q0:q1], k_nope[:, :q1])
              + jnp.einsum("bqhr,bkr->bhqk", q_rope[:, q0:q1], k_rope[:, :q1]))
        sc = sc.astype(jnp.float32) * scale
        qpos = q0 + jnp.arange(Q_BLOCK)[:, None]
        kpos = jnp.arange(q1)[None, :]
        sc = jnp.where(kpos <= qpos, sc, -jnp.inf)
        p = jax.nn.softmax(sc, axis=-1).astype(v.dtype)
        outs.append(jnp.einsum("bhqk,bkhd->bqhd", p, v[:, :q1]))
    o = jnp.concatenate(outs, axis=1)
    return o.reshape(b, s, MLA_HEADS * V_HEAD_DIM)


def conformer_conv(a, g, dw_w, dw_b, ln_g, ln_b):
    u = a * jax.nn.sigmoid(g)
    u = causal_depthwise_conv(u, dw_w) + dw_b
    u = layer_norm(u, ln_g, ln_b)
    return jax.nn.silu(u)


def short_conv(b_gate, c_gate, xs, w):
    return b_gate * causal_depthwise_conv(c_gate * xs, w)


def setup_inputs(seed: int = 0) -> dict:
    key = jax.random.key(seed)
    ks = jax.random.split(key, 24)
    f32 = jnp.float32
    L = DEPTH

    def w(k, shape, fan_in):
        return jax.random.normal(k, shape, f32) * (fan_in ** -0.5)

    def gain(k, shape):
        return 1.0 + 0.02 * jax.random.normal(k, shape, f32)

    def small(k, shape):
        return 0.02 * jax.random.normal(k, shape, f32)

    return {
        "x": jax.random.normal(ks[0], (BATCH, SEQ, D_MODEL), f32),
        "ffn1_norm": gain(ks[1], (L, D_MODEL)),
        "ffn1_w_gate": w(ks[2], (L, D_MODEL, D_FF), D_MODEL),
        "ffn1_w_up": w(ks[3], (L, D_MODEL, D_FF), D_MODEL),
        "ffn1_w_down": w(ks[4], (L, D_FF, D_MODEL), D_FF),
        "mix_norm": gain(ks[5], (L, D_MODEL)),
        "w_in": w(ks[6], (L, D_MODEL, IN_WIDTH), D_MODEL),
        "q_norm": gain(ks[7], (L, Q_LORA_RANK)),
        "w_uq": w(ks[8], (L, Q_LORA_RANK, MLA_HEADS * (QK_NOPE_DIM + QK_ROPE_DIM)), Q_LORA_RANK),
        "kv_norm": gain(ks[9], (L, KV_LORA_RANK)),
        "w_ukv": w(ks[10], (L, KV_LORA_RANK, MLA_HEADS * (QK_NOPE_DIM + V_HEAD_DIM)), KV_LORA_RANK),
        "conf_dw_w": w(ks[11], (L, CONF_KERNEL, CONF_WIDTH), CONF_KERNEL),
        "conf_dw_b": small(ks[12], (L, CONF_WIDTH)),
        "conf_ln_g": gain(ks[13], (L, CONF_WIDTH)),
        "conf_ln_b": small(ks[14], (L, CONF_WIDTH)),
        "sc_w": w(ks[15], (L, SC_KERNEL, SC_WIDTH), SC_KERNEL),
        "mix_out_norm": gain(ks[16], (L, N_GROUPS, GROUP_DIM)),
        "w_o": w(ks[17], (L, D_MIX, D_MODEL), D_MIX),
        "ffn2_norm": gain(ks[18], (L, D_MODEL)),
        "ffn2_w_gate": w(ks[19], (L, D_MODEL, D_FF), D_MODEL),
        "ffn2_w_up": w(ks[20], (L, D_MODEL, D_FF), D_MODEL),
        "ffn2_w_down": w(ks[21], (L, D_FF, D_MODEL), D_FF),
        "final_norm": gain(ks[22], (D_MODEL,)),
    }


def reference(x, ffn1_norm, ffn1_w_gate, ffn1_w_up, ffn1_w_down, mix_norm, w_in,
              q_norm, w_uq, kv_norm, w_ukv, conf_dw_w, conf_dw_b, conf_ln_g, conf_ln_b,
              sc_w, mix_out_norm, w_o, ffn2_norm, ffn2_w_gate, ffn2_w_up, ffn2_w_down,
              final_norm):
    b, s, _ = x.shape
    cos, sin = rope_tables(s)
    widths = [Q_LORA_RANK, KV_LORA_RANK, QK_ROPE_DIM, CONF_WIDTH, CONF_WIDTH, SC_WIDTH, SC_WIDTH, SC_WIDTH]
    split_at = [int(v) for v in np.cumsum(widths)[:-1]]
    h = x
    for l in range(DEPTH):
        h = h + 0.5 * swiglu(rms_norm(h, ffn1_norm[l]), ffn1_w_gate[l], ffn1_w_up[l], ffn1_w_down[l])
        z = rms_norm(h, mix_norm[l]) @ w_in[l]
        c_q, c_kv, k_rope, conf_a, conf_g, sc_b, sc_c, sc_x = jnp.split(z, split_at, axis=-1)
        o_mla = mla(c_q, c_kv, k_rope, q_norm[l], w_uq[l], kv_norm[l], w_ukv[l], cos, sin)
        o_conf = conformer_conv(conf_a, conf_g, conf_dw_w[l], conf_dw_b[l], conf_ln_g[l], conf_ln_b[l])
        o_sc = short_conv(sc_b, sc_c, sc_x, sc_w[l])
        y = jnp.concatenate([o_mla, o_conf, o_sc], axis=-1).reshape(b, s, N_GROUPS, GROUP_DIM)
        y = rms_norm(y, mix_out_norm[l]).reshape(b, s, D_MIX)
        h = h + y @ w_o[l]
        h = h + 0.5 * swiglu(rms_norm(h, ffn2_norm[l]), ffn2_w_gate[l], ffn2_w_up[l], ffn2_w_down[l])
    return rms_norm(h, final_norm)
```

```python
import functools

import jax
import jax.numpy as jnp
from jax import lax
from jax.experimental import pallas as pl
from jax.experimental.pallas import tpu as pltpu

F32 = jnp.float32
BF16 = jnp.bfloat16

LANES = 128
SUBLANES = 8
VMEM_PHYSICAL_BYTES = 64 * 1024 * 1024

GROUP_DIM = 128
QK_NOPE_DIM = 128
QK_ROPE_DIM = 64
V_HEAD_DIM = 128
QK_PAD_DIM = 256
ROPE_THETA = 10000.0
CONF_KERNEL = 31
SC_KERNEL = 3
RMS_EPS = 1e-6
LN_EPS = 1e-5

CONF_PAD_ROWS = 32
SC_PAD_ROWS = 8


def _vmem_limit(estimate_bytes):
    return int(min(estimate_bytes * 5 // 4 + (4 << 20), VMEM_PHYSICAL_BYTES - (2 << 20)))


def _rms(x, gain):
    return x * lax.rsqrt(jnp.mean(x * x, axis=-1, keepdims=True) + RMS_EPS) * gain


def _ffn_body(x_ref, g_ref, wg_ref, wu_ref, wd_ref, fg_ref, o_ref, n_ref, *, final_norm):
    f = pl.program_id(1)

    @pl.when(f == 0)
    def _():
        x = x_ref[...]
        n_ref[...] = _rms(x, g_ref[...]).astype(BF16)
        o_ref[...] = x

    n = n_ref[...]
    gate = jnp.dot(n, wg_ref[...], preferred_element_type=F32)
    up = jnp.dot(n, wu_ref[...], preferred_element_type=F32)
    act = (gate * jax.nn.sigmoid(gate) * up * 0.5).astype(BF16)
    o_ref[...] += jnp.dot(act, wd_ref[...], preferred_element_type=F32)

    if final_norm:
        @pl.when(f == pl.num_programs(1) - 1)
        def _():
            o_ref[...] = _rms(o_ref[...], fg_ref[...])


def _ffn(h, gain, wg, wu, wd, final_gain, *, final_norm, tm=1024, tf=512):
    m, d = h.shape
    ff = wg.shape[1]
    tm = min(tm, m)
    assert m % tm == 0 and ff % tf == 0
    est = (2 * tm * d * 4) * 2 + tm * d * 2 + 2 * (3 * d * tf * 2) + tm * tf * (4 + 4 + 2)
    return pl.pallas_call(
        functools.partial(_ffn_body, final_norm=final_norm),
        grid=(m // tm, ff // tf),
        in_specs=[
            pl.BlockSpec((tm, d), lambda i, f: (i, 0)),
            pl.BlockSpec((1, d), lambda i, f: (0, 0)),
            pl.BlockSpec((d, tf), lambda i, f: (0, f)),
            pl.BlockSpec((d, tf), lambda i, f: (0, f)),
            pl.BlockSpec((tf, d), lambda i, f: (f, 0)),
            pl.BlockSpec((1, d), lambda i, f: (0, 0)),
        ],
        out_specs=pl.BlockSpec((tm, d), lambda i, f: (i, 0)),
        out_shape=jax.ShapeDtypeStruct((m, d), F32),
        scratch_shapes=[pltpu.VMEM((tm, d), BF16)],
        compiler_params=pltpu.CompilerParams(
            dimension_semantics=("arbitrary", "arbitrary"),
            vmem_limit_bytes=_vmem_limit(est)),
        name="ffn_final" if final_norm else "ffn",
    )(h, gain.reshape(1, d), wg, wu, wd, final_gain.reshape(1, d))


def _inproj_body(x_ref, g_ref, w_ref, z_ref):
    n = _rms(x_ref[...], g_ref[...]).astype(BF16)
    z_ref[...] = jnp.dot(n, w_ref[...], preferred_element_type=F32).astype(BF16)


def _inproj(h, gain, w, *, tm=512):
    m, d = h.shape
    nw = w.shape[1]
    tm = min(tm, m)
    assert m % tm == 0
    est = 2 * tm * d * 4 + 2 * d * nw * 2 + 2 * tm * nw * 2 + tm * nw * 4 + tm * d * 2
    return pl.pallas_call(
        _inproj_body,
        grid=(m // tm,),
        in_specs=[
            pl.BlockSpec((tm, d), lambda i: (i, 0)),
            pl.BlockSpec((1, d), lambda i: (0, 0)),
            pl.BlockSpec((d, nw), lambda i: (0, 0)),
        ],
        out_specs=pl.BlockSpec((tm, nw), lambda i: (i, 0)),
        out_shape=jax.ShapeDtypeStruct((m, nw), BF16),
        compiler_params=pltpu.CompilerParams(
            dimension_semantics=("arbitrary",), vmem_limit_bytes=_vmem_limit(est)),
        name="inproj",
    )(h, gain.reshape(1, d), w)


def _rope128(x, c_tab, s_lo, s_hi):
    half = QK_ROPE_DIM // 2
    partner_hi = pltpu.roll(x, LANES - half, 1)
    partner_lo = pltpu.roll(x, half, 1)
    return x * c_tab + partner_hi * s_lo + partner_lo * s_hi


def _mla_up_body(cq_ref, ckv_ref, kr_ref, qg_ref, kvg_ref, wq_ref, wkv_ref,
                 c_ref, slo_ref, shi_ref, q_ref, k_ref, v_ref, *, heads):
    c_tab, s_lo, s_hi = c_ref[...], slo_ref[...], shi_ref[...]

    nq = _rms(cq_ref[...].astype(F32), qg_ref[...]).astype(BF16)
    q = jnp.dot(nq, wq_ref[...], preferred_element_type=F32)
    nkv = _rms(ckv_ref[...].astype(F32), kvg_ref[...]).astype(BF16)
    kv = jnp.dot(nkv, wkv_ref[...], preferred_element_type=F32)
    k_rope = _rope128(kr_ref[...].astype(F32), c_tab, s_lo, s_hi).astype(BF16)

    for hd in range(heads):
        base = hd * QK_PAD_DIM
        q_ref[:, base:base + LANES] = q[:, base:base + LANES].astype(BF16)
        q_ref[:, base + LANES:base + 2 * LANES] = _rope128(
            q[:, base + LANES:base + 2 * LANES], c_tab, s_lo, s_hi).astype(BF16)
        k_ref[:, base:base + LANES] = kv[:, base:base + LANES].astype(BF16)
        k_ref[:, base + LANES:base + 2 * LANES] = k_rope
        v_ref[:, hd * LANES:(hd + 1) * LANES] = kv[:, base + LANES:base + 2 * LANES].astype(BF16)


def _mla_up(z, q_gain, kv_gain, wq, wkv, c_tab, s_lo, s_hi, *, seq, heads, q_rank, kv_rank, tm=512):
    m, nw = z.shape
    tm = min(tm, seq)
    assert m % tm == 0 and seq % tm == 0 and q_rank == kv_rank
    s_blocks = seq // tm
    kr_block = (nw - LANES) // LANES
    hq = heads * QK_PAD_DIM
    hv = heads * V_HEAD_DIM
    est = (2 * (2 * tm * q_rank * 2 + tm * LANES * 2) + 2 * 2 * q_rank * hq * 2
           + 2 * 3 * tm * LANES * 4 + 2 * tm * (2 * hq + hv) * 2 + 2 * tm * hq * 4)
    tab_spec = pl.BlockSpec((tm, LANES), lambda i: (i % s_blocks, 0))
    return pl.pallas_call(
        functools.partial(_mla_up_body, heads=heads),
        grid=(m // tm,),
        in_specs=[
            pl.BlockSpec((tm, q_rank), lambda i: (i, 0)),
            pl.BlockSpec((tm, kv_rank), lambda i: (i, 1)),
            pl.BlockSpec((tm, LANES), lambda i: (i, kr_block)),
            pl.BlockSpec((1, q_rank), lambda i: (0, 0)),
            pl.BlockSpec((1, kv_rank), lambda i: (0, 0)),
            pl.BlockSpec((q_rank, hq), lambda i: (0, 0)),
            pl.BlockSpec((kv_rank, hq), lambda i: (0, 0)),
            tab_spec, tab_spec, tab_spec,
        ],
        out_specs=[
            pl.BlockSpec((tm, hq), lambda i: (i, 0)),
            pl.BlockSpec((tm, hq), lambda i: (i, 0)),
            pl.BlockSpec((tm, hv), lambda i: (i, 0)),
        ],
        out_shape=[
            jax.ShapeDtypeStruct((m, hq), BF16),
            jax.ShapeDtypeStruct((m, hq), BF16),
            jax.ShapeDtypeStruct((m, hv), BF16),
        ],
        compiler_params=pltpu.CompilerParams(
            dimension_semantics=("arbitrary",), vmem_limit_bytes=_vmem_limit(est)),
        name="mla_up",
    )(z, z, z, q_gain.reshape(1, q_rank), kv_gain.reshape(1, kv_rank), wq, wkv, c_tab, s_lo, s_hi)


def _attn_body(q_ref, k_ref, v_ref, g_ref, o_ref, *, seq, tq, scale):
    gain = g_ref[0]
    for qi in range(seq // tq):
        kend = (qi + 1) * tq
        q = q_ref[qi * tq:kend, :]
        s = lax.dot_general(q, k_ref[0:kend, :], (((1,), (1,)), ((), ())),
                            preferred_element_type=F32) * scale
        qpos = qi * tq + lax.broadcasted_iota(jnp.int32, (tq, kend), 0)
        kpos = lax.broadcasted_iota(jnp.int32, (tq, kend), 1)
        s = jnp.where(kpos <= qpos, s, -jnp.inf)
        p = jnp.exp(s - jnp.max(s, axis=-1, keepdims=True))
        denom = jnp.sum(p, axis=-1, keepdims=True)
        o = jnp.dot(p.astype(BF16), v_ref[0:kend, :], preferred_element_type=F32) / denom
        o_ref[qi * tq:kend, :] = _rms(o, gain).astype(BF16)


def _attention(q, k, v, gains, *, batch, seq, heads, tq=512):
    tq = min(tq, seq)
    assert seq % tq == 0
    scale = float(QK_NOPE_DIM + QK_ROPE_DIM) ** -0.5
    q3 = q.reshape(batch, seq, heads * QK_PAD_DIM)
    k3 = k.reshape(batch, seq, heads * QK_PAD_DIM)
    v3 = v.reshape(batch, seq, heads * V_HEAD_DIM)
    est = 2 * (2 * seq * QK_PAD_DIM * 2 + 2 * seq * V_HEAD_DIM * 2) + 3 * tq * seq * 4
    return pl.pallas_call(
        functools.partial(_attn_body, seq=seq, tq=tq, scale=scale),
        grid=(batch, heads),
        in_specs=[
            pl.BlockSpec((None, seq, QK_PAD_DIM), lambda b, h: (b, 0, h)),
            pl.BlockSpec((None, seq, QK_PAD_DIM), lambda b, h: (b, 0, h)),
            pl.BlockSpec((None, seq, V_HEAD_DIM), lambda b, h: (b, 0, h)),
            pl.BlockSpec((1, 1, GROUP_DIM), lambda b, h: (h, 0, 0)),
        ],
        out_specs=pl.BlockSpec((None, seq, V_HEAD_DIM), lambda b, h: (b, 0, h)),
        out_shape=jax.ShapeDtypeStruct((batch, seq, heads * V_HEAD_DIM), BF16),
        compiler_params=pltpu.CompilerParams(
            dimension_semantics=("arbitrary", "arbitrary"), vmem_limit_bytes=_vmem_limit(est)),
        name="attention",
    )(q3, k3, v3, gains.reshape(-1, 1, GROUP_DIM))


def _conv_body(a_ref, g_ref, sb_ref, scg_ref, sx_ref, cw_ref, cb_ref, lg_ref, lb_ref, sw_ref, og_ref,
               o_ref, u_scr, m_scr, y_scr, *, seq, rows, width):
    groups = width // LANES
    n_chunks = seq // rows

    u_scr[0:CONF_PAD_ROWS, :] = jnp.zeros((CONF_PAD_ROWS, width), F32)
    m_scr[0:SC_PAD_ROWS, :] = jnp.zeros((SC_PAD_ROWS, width), F32)

    def fill(c, carry):
        t0 = pl.multiple_of(c * rows, rows)
        a = a_ref[pl.ds(t0, rows), :].astype(F32)
        g = g_ref[pl.ds(t0, rows), :].astype(F32)
        u_scr[pl.ds(CONF_PAD_ROWS + t0, rows), :] = a * jax.nn.sigmoid(g)
        m_scr[pl.ds(SC_PAD_ROWS + t0, rows), :] = (
            scg_ref[pl.ds(t0, rows), :].astype(F32) * sx_ref[pl.ds(t0, rows), :].astype(F32))
        return carry

    lax.fori_loop(0, n_chunks, fill, 0)

    def causal_taps(src_ref, w_ref, t0, lanes, pad_rows, taps, acc):
        win_rows = rows + pad_rows
        win = src_ref[pl.ds(t0, win_rows), lanes]
        for r in range(SUBLANES):
            shifted = None
            for j in range(taps):
                off = pad_rows - (taps - 1) + j
                if off % SUBLANES != r:
                    continue
                if shifted is None:
                    shifted = win if r == 0 else pltpu.roll(win, win_rows - r, 0)
                base = off - r
                acc = acc + w_ref[j:j + 1, lanes] * shifted[base:base + rows]
        return acc

    def chunk(c, carry):
        t0 = pl.multiple_of(c * rows, rows)
        for cg in range(groups):
            lanes = slice(cg * LANES, (cg + 1) * LANES)
            acc = jnp.broadcast_to(cb_ref[:, lanes], (rows, LANES))
            y_scr[:, lanes] = causal_taps(u_scr, cw_ref, t0, lanes, CONF_PAD_ROWS, CONF_KERNEL, acc)
        y = y_scr[...]
        mu = jnp.mean(y, axis=-1, keepdims=True)
        dlt = y - mu
        var = jnp.mean(dlt * dlt, axis=-1, keepdims=True)
        ln = dlt * lax.rsqrt(var + LN_EPS) * lg_ref[...] + lb_ref[...]
        conf = ln * jax.nn.sigmoid(ln)
        for cg in range(groups):
            lanes = slice(cg * LANES, (cg + 1) * LANES)
            o_ref[pl.ds(t0, rows), lanes] = _rms(conf[:, lanes], og_ref[cg:cg + 1, :]).astype(BF16)
        for cg in range(groups):
            lanes = slice(cg * LANES, (cg + 1) * LANES)
            acc = causal_taps(m_scr, sw_ref, t0, lanes, SC_PAD_ROWS, SC_KERNEL,
                              jnp.zeros((rows, LANES), F32))
            sc = sb_ref[pl.ds(t0, rows), lanes].astype(F32) * acc
            o_ref[pl.ds(t0, rows), width + cg * LANES:width + (cg + 1) * LANES] = _rms(
                sc, og_ref[groups + cg:groups + cg + 1, :]).astype(BF16)
        return carry

    lax.fori_loop(0, n_chunks, chunk, 0)


def _conv_mixers(z, conf_w, conf_b, ln_g, ln_b, sc_w, gains, *, batch, seq, width, first_block, rows=128):
    nw = z.shape[1]
    z3 = z.reshape(batch, seq, nw)
    rows = min(rows, seq)
    assert seq % rows == 0 and conf_w.shape == (CONF_KERNEL, width) and sc_w.shape == (SC_KERNEL, width)
    groups = width // LANES

    def zspec(j):
        return pl.BlockSpec((None, seq, width), lambda b: (b, 0, first_block + j))

    def full(shape):
        return pl.BlockSpec(shape, lambda b: (0,) * len(shape))

    est = (2 * 5 * seq * width * 2 + 2 * seq * 2 * width * 2
           + (2 * seq + CONF_PAD_ROWS + SC_PAD_ROWS + rows) * width * 4)
    return pl.pallas_call(
        functools.partial(_conv_body, seq=seq, rows=rows, width=width),
        grid=(batch,),
        in_specs=[zspec(0), zspec(1), zspec(2), zspec(3), zspec(4),
                  full((CONF_KERNEL, width)), full((1, width)), full((1, width)), full((1, width)),
                  full((SC_KERNEL, width)), full((2 * groups, GROUP_DIM))],
        out_specs=pl.BlockSpec((None, seq, 2 * width), lambda b: (b, 0, 0)),
        out_shape=jax.ShapeDtypeStruct((batch, seq, 2 * width), BF16),
        scratch_shapes=[pltpu.VMEM((CONF_PAD_ROWS + seq, width), F32),
                        pltpu.VMEM((SC_PAD_ROWS + seq, width), F32),
                        pltpu.VMEM((rows, width), F32)],
        compiler_params=pltpu.CompilerParams(
            dimension_semantics=("arbitrary",), vmem_limit_bytes=_vmem_limit(est)),
        name="conv_mixers",
    )(z3, z3, z3, z3, z3, conf_w, conf_b.reshape(1, width), ln_g.reshape(1, width),
      ln_b.reshape(1, width), sc_w, gains)


def _outproj_body(ya_ref, yc_ref, wa_ref, wc_ref, h_ref, o_ref):
    o_ref[...] = (h_ref[...]
                  + jnp.dot(ya_ref[...], wa_ref[...], preferred_element_type=F32)
                  + jnp.dot(yc_ref[...], wc_ref[...], preferred_element_type=F32))


def _outproj(y_attn, y_conv, w_o, h, *, tm=512):
    m, d = h.shape
    ka, kc = y_attn.shape[1], y_conv.shape[1]
    tm = min(tm, m)
    assert m % tm == 0 and ka == kc and ka + kc == w_o.shape[0]
    est = 2 * tm * (ka + kc) * 2 + 2 * (ka + kc) * d * 2 + 4 * tm * d * 4 + tm * d * 4
    return pl.pallas_call(
        _outproj_body,
        grid=(m // tm,),
        in_specs=[
            pl.BlockSpec((tm, ka), lambda i: (i, 0)),
            pl.BlockSpec((tm, kc), lambda i: (i, 0)),
            pl.BlockSpec((ka, d), lambda i: (0, 0)),
            pl.BlockSpec((kc, d), lambda i: (1, 0)),
            pl.BlockSpec((tm, d), lambda i: (i, 0)),
        ],
        out_specs=pl.BlockSpec((tm, d), lambda i: (i, 0)),
        out_shape=jax.ShapeDtypeStruct((m, d), F32),
        compiler_params=pltpu.CompilerParams(
            dimension_semantics=("arbitrary",), vmem_limit_bytes=_vmem_limit(est)),
        name="outproj",
    )(y_attn, y_conv, w_o, w_o, h)


def _rope_tables(seq):
    half = QK_ROPE_DIM // 2
    inv = 1.0 / (ROPE_THETA ** (jnp.arange(0, QK_ROPE_DIM, 2, dtype=F32) / QK_ROPE_DIM))
    ang = jnp.arange(seq, dtype=F32)[:, None] * inv[None, :]
    cos, sin = jnp.cos(ang), jnp.sin(ang)
    zero = jnp.zeros((seq, half), F32)
    c_tab = jnp.concatenate([cos, cos, zero, zero], axis=-1)
    s_lo = jnp.concatenate([-sin, zero, zero, zero], axis=-1)
    s_hi = jnp.concatenate([zero, sin, zero, zero], axis=-1)
    return c_tab, s_lo, s_hi


def kernel(x, ffn1_norm, ffn1_w_gate, ffn1_w_up, ffn1_w_down, mix_norm, w_in, q_norm, w_uq, kv_norm, w_ukv, conf_dw_w, conf_dw_b, conf_ln_g, conf_ln_b, sc_w, mix_out_norm, w_o, ffn2_norm, ffn2_w_gate, ffn2_w_up, ffn2_w_down, final_norm):
    batch, seq, d = x.shape
    depth = w_in.shape[0]
    q_rank = q_norm.shape[1]
    kv_rank = kv_norm.shape[1]
    conf_width = conf_dw_w.shape[2]
    sc_width = sc_w.shape[2]
    heads = w_ukv.shape[2] // (QK_NOPE_DIM + V_HEAD_DIM)
    assert w_uq.shape[2] == heads * (QK_NOPE_DIM + QK_ROPE_DIM)
    assert conf_width == sc_width == q_rank == kv_rank
    assert w_in.shape[2] == q_rank + kv_rank + QK_ROPE_DIM + 2 * conf_width + 3 * sc_width
    assert heads * V_HEAD_DIM + conf_width + sc_width == w_o.shape[1]

    c_tab, s_lo, s_hi = _rope_tables(seq)
    h = x.reshape(batch * seq, d)
    rope_at = q_rank + kv_rank
    for l in range(depth):
        wi = w_in[l]
        w_in_p = jnp.concatenate(
            [wi[:, :rope_at], wi[:, rope_at + QK_ROPE_DIM:], wi[:, rope_at:rope_at + QK_ROPE_DIM],
             jnp.zeros((d, LANES - QK_ROPE_DIM), wi.dtype)], axis=1).astype(BF16)
        wq = w_uq[l].reshape(q_rank, heads, QK_NOPE_DIM + QK_ROPE_DIM)
        wq_p = jnp.pad(wq, ((0, 0), (0, 0), (0, QK_PAD_DIM - QK_NOPE_DIM - QK_ROPE_DIM))).reshape(
            q_rank, heads * QK_PAD_DIM).astype(BF16)
        wkv = w_ukv[l].astype(BF16)

        h = _ffn(h, ffn1_norm[l], ffn1_w_gate[l].astype(BF16), ffn1_w_up[l].astype(BF16),
                 ffn1_w_down[l].astype(BF16), final_norm, final_norm=False)
        z = _inproj(h, mix_norm[l], w_in_p)
        q, k, v = _mla_up(z, q_norm[l], kv_norm[l], wq_p, wkv, c_tab, s_lo, s_hi,
                          seq=seq, heads=heads, q_rank=q_rank, kv_rank=kv_rank)
        y_attn = _attention(q, k, v, mix_out_norm[l, :heads], batch=batch, seq=seq, heads=heads)
        y_conv = _conv_mixers(z, conf_dw_w[l], conf_dw_b[l], conf_ln_g[l], conf_ln_b[l], sc_w[l],
                              mix_out_norm[l, heads:], batch=batch, seq=seq, width=conf_width,
                              first_block=(q_rank + kv_rank) // conf_width)
        h = _outproj(y_attn.reshape(batch * seq, -1), y_conv.reshape(batch * seq, -1),
                     w_o[l].astype(BF16), h)
        h = _ffn(h, ffn2_norm[l], ffn2_w_gate[l].astype(BF16), ffn2_w_up[l].astype(BF16),
                 ffn2_w_down[l].astype(BF16), final_norm, final_norm=(l == depth - 1))
    return h.reshape(batch, seq, d)
```

```python
import functools

import jax
import jax.numpy as jnp
from jax import lax
from jax.experimental import pallas as pl
from jax.experimental.pallas import tpu as pltpu

F32 = jnp.float32
BF16 = jnp.bfloat16

LANES = 128
SUBLANES = 8
VMEM_PHYSICAL_BYTES = 64 * 1024 * 1024

GROUP_DIM = 128
QK_NOPE_DIM = 128
QK_ROPE_DIM = 64
V_HEAD_DIM = 128
QK_PAD_DIM = 256
ROPE_THETA = 10000.0
CONF_KERNEL = 31
SC_KERNEL = 3
RMS_EPS = 1e-6
LN_EPS = 1e-5
LOG2_E = 1.4426950408889634

CONF_PAD_ROWS = 32
SC_PAD_ROWS = 8


def _vmem_limit(estimate_bytes):
    return int(min(estimate_bytes * 5 // 4 + (4 << 20), VMEM_PHYSICAL_BYTES - (2 << 20)))


def _rms(x, gain):
    return x * lax.rsqrt(jnp.mean(x * x, axis=-1, keepdims=True) + RMS_EPS) * gain


def _ffn_step(x_ref, g_ref, load_weights, fg_ref, o_ref, n_ref, final_norm):
    f = pl.program_id(1)

    @pl.when(f == 0)
    def _():
        x = x_ref[...]
        n_ref[...] = _rms(x, g_ref[...]).astype(BF16)
        o_ref[...] = x

    wg, wu, wd = load_weights()
    n = n_ref[...]
    gate = jnp.dot(n, wg, preferred_element_type=F32)
    up = jnp.dot(n, wu, preferred_element_type=F32)
    act = (gate * jax.nn.sigmoid(gate) * up * 0.5).astype(BF16)
    o_ref[...] += jnp.dot(act, wd, preferred_element_type=F32)

    if final_norm:
        @pl.when(f == pl.num_programs(1) - 1)
        def _():
            o_ref[...] = _rms(o_ref[...], fg_ref[...])


def _ffn_first_body(x_ref, g_ref, wg_ref, wu_ref, wd_ref, fg_ref, o_ref, wgo_ref, wuo_ref, wdo_ref, n_ref,
                    *, final_norm):
    def load_weights():
        wg = wg_ref[...].astype(BF16)
        wu = wu_ref[...].astype(BF16)
        wd = wd_ref[...].astype(BF16)
        wgo_ref[...] = wg
        wuo_ref[...] = wu
        wdo_ref[...] = wd
        return wg, wu, wd

    _ffn_step(x_ref, g_ref, load_weights, fg_ref, o_ref, n_ref, final_norm)


def _ffn_rest_body(x_ref, g_ref, wg_ref, wu_ref, wd_ref, fg_ref, first_ref, o_ref, n_ref, *, final_norm):
    del first_ref
    _ffn_step(x_ref, g_ref, lambda: (wg_ref[...], wu_ref[...], wd_ref[...]), fg_ref, o_ref, n_ref,
              final_norm)


def _ffn(h, gains, w_gate, w_up, w_down, final_gain, *, layer, final_norm, tm=1024, tf=512, tf_first=256):
    m, d = h.shape
    ff = w_gate.shape[2]
    tm = min(tm, m)
    assert m % tm == 0 and ff % tf == 0 and ff % tf_first == 0
    name = "ffn_final" if final_norm else "ffn"
    vec = pl.BlockSpec((None, 1, d), lambda i, f: (layer, 0, 0))
    fvec = pl.BlockSpec((1, d), lambda i, f: (0, 0))
    gains3 = gains.reshape(gains.shape[0], 1, d)
    fg = final_gain.reshape(1, d)
    scratch = [pltpu.VMEM((tm, d), BF16)]
    sem = ("arbitrary", "arbitrary")

    est = (4 * tm * d * 4 + tm * d * 2 + 2 * 3 * d * tf_first * (4 + 2) + 3 * d * tf_first * 2
           + tm * tf_first * (4 + 4 + 2))
    first, wg, wu, wd = pl.pallas_call(
        functools.partial(_ffn_first_body, final_norm=final_norm),
        grid=(1, ff // tf_first),
        in_specs=[
            pl.BlockSpec((tm, d), lambda i, f: (0, 0)),
            vec,
            pl.BlockSpec((None, d, tf_first), lambda i, f: (layer, 0, f)),
            pl.BlockSpec((None, d, tf_first), lambda i, f: (layer, 0, f)),
            pl.BlockSpec((None, tf_first, d), lambda i, f: (layer, f, 0)),
            fvec,
        ],
        out_specs=[
            pl.BlockSpec((tm, d), lambda i, f: (0, 0)),
            pl.BlockSpec((d, tf_first), lambda i, f: (0, f)),
            pl.BlockSpec((d, tf_first), lambda i, f: (0, f)),
            pl.BlockSpec((tf_first, d), lambda i, f: (f, 0)),
        ],
        out_shape=[
            jax.ShapeDtypeStruct((m, d), F32),
            jax.ShapeDtypeStruct((d, ff), BF16),
            jax.ShapeDtypeStruct((d, ff), BF16),
            jax.ShapeDtypeStruct((ff, d), BF16),
        ],
        scratch_shapes=scratch,
        compiler_params=pltpu.CompilerParams(dimension_semantics=sem, vmem_limit_bytes=_vmem_limit(est)),
        name=name + "_first",
    )(h, gains3, w_gate, w_up, w_down, fg)
    if m == tm:
        return first

    est = 4 * tm * d * 4 + tm * d * 2 + 2 * 3 * d * tf * 2 + tm * tf * (4 + 4 + 2)
    return pl.pallas_call(
        functools.partial(_ffn_rest_body, final_norm=final_norm),
        grid=(m // tm - 1, ff // tf),
        in_specs=[
            pl.BlockSpec((tm, d), lambda i, f: (i + 1, 0)),
            vec,
            pl.BlockSpec((d, tf), lambda i, f: (0, f)),
            pl.BlockSpec((d, tf), lambda i, f: (0, f)),
            pl.BlockSpec((tf, d), lambda i, f: (f, 0)),
            fvec,
            pl.BlockSpec(memory_space=pl.ANY),
        ],
        out_specs=pl.BlockSpec((tm, d), lambda i, f: (i + 1, 0)),
        out_shape=jax.ShapeDtypeStruct((m, d), F32),
        input_output_aliases={6: 0},
        scratch_shapes=scratch,
        compiler_params=pltpu.CompilerParams(dimension_semantics=sem, vmem_limit_bytes=_vmem_limit(est)),
        name=name,
    )(h, gains3, wg, wu, wd, fg, first)


def _inproj_body(x_ref, g_ref, w_ref, z_ref):
    n = _rms(x_ref[...], g_ref[...]).astype(BF16)
    z_ref[...] = jnp.dot(n, w_ref[...], preferred_element_type=F32).astype(BF16)


def _inproj(h, gains, w, *, layer, tm=512):
    m, d = h.shape
    nw = w.shape[2]
    tm = min(tm, m)
    assert m % tm == 0
    est = 2 * tm * d * 4 + 2 * d * nw * 2 + 2 * tm * nw * 2 + tm * nw * 4 + tm * d * 2
    return pl.pallas_call(
        _inproj_body,
        grid=(m // tm,),
        in_specs=[
            pl.BlockSpec((tm, d), lambda i: (i, 0)),
            pl.BlockSpec((None, 1, d), lambda i: (layer, 0, 0)),
            pl.BlockSpec((None, d, nw), lambda i: (layer, 0, 0)),
        ],
        out_specs=pl.BlockSpec((tm, nw), lambda i: (i, 0)),
        out_shape=jax.ShapeDtypeStruct((m, nw), BF16),
        compiler_params=pltpu.CompilerParams(
            dimension_semantics=("arbitrary",), vmem_limit_bytes=_vmem_limit(est)),
        name="inproj",
    )(h, gains.reshape(gains.shape[0], 1, d), w)


def _rope128(x, c_tab, s_lo, s_hi):
    half = QK_ROPE_DIM // 2
    partner_hi = pltpu.roll(x, LANES - half, 1)
    partner_lo = pltpu.roll(x, half, 1)
    return x * c_tab + partner_hi * s_lo + partner_lo * s_hi


def _mla_up_body(cq_ref, ckv_ref, kr_ref, qg_ref, kvg_ref, wq_ref, wkv_ref,
                 c_ref, slo_ref, shi_ref, q_ref, k_ref, v_ref, *, heads):
    c_tab, s_lo, s_hi = c_ref[...], slo_ref[...], shi_ref[...]

    scale = float(QK_NOPE_DIM + QK_ROPE_DIM) ** -0.5 * LOG2_E
    nq = _rms(cq_ref[...].astype(F32), qg_ref[...] * scale).astype(BF16)
    q = jnp.dot(nq, wq_ref[...], preferred_element_type=F32)
    nkv = _rms(ckv_ref[...].astype(F32), kvg_ref[...]).astype(BF16)
    kv = jnp.dot(nkv, wkv_ref[...], preferred_element_type=F32)
    k_rope = _rope128(kr_ref[...].astype(F32), c_tab, s_lo, s_hi).astype(BF16)

    for hd in range(heads):
        base = hd * QK_PAD_DIM
        q_ref[:, base:base + LANES] = q[:, base:base + LANES].astype(BF16)
        q_ref[:, base + LANES:base + 2 * LANES] = _rope128(
            q[:, base + LANES:base + 2 * LANES], c_tab, s_lo, s_hi).astype(BF16)
        k_ref[:, base:base + LANES] = kv[:, base:base + LANES].astype(BF16)
        k_ref[:, base + LANES:base + 2 * LANES] = k_rope
        v_ref[:, hd * LANES:(hd + 1) * LANES] = kv[:, base + LANES:base + 2 * LANES].astype(BF16)


def _mla_up(z, q_gains, kv_gains, wq, wkv, c_tab, s_lo, s_hi, *, layer, seq, heads, q_rank, kv_rank,
            tm=512):
    m, nw = z.shape
    depth = q_gains.shape[0]
    tm = min(tm, seq)
    assert m % tm == 0 and seq % tm == 0 and q_rank == kv_rank
    s_blocks = seq // tm
    kr_block = (nw - LANES) // LANES
    hq = heads * QK_PAD_DIM
    hv = heads * V_HEAD_DIM
    est = (2 * (2 * tm * q_rank * 2 + tm * LANES * 2) + 2 * 2 * q_rank * hq * 2
           + 2 * 3 * tm * LANES * 4 + 2 * tm * (2 * hq + hv) * 2 + 2 * tm * hq * 4)
    tab_spec = pl.BlockSpec((tm, LANES), lambda i: (i % s_blocks, 0))
    return pl.pallas_call(
        functools.partial(_mla_up_body, heads=heads),
        grid=(m // tm,),
        in_specs=[
            pl.BlockSpec((tm, q_rank), lambda i: (i, 0)),
            pl.BlockSpec((tm, kv_rank), lambda i: (i, 1)),
            pl.BlockSpec((tm, LANES), lambda i: (i, kr_block)),
            pl.BlockSpec((None, 1, q_rank), lambda i: (layer, 0, 0)),
            pl.BlockSpec((None, 1, kv_rank), lambda i: (layer, 0, 0)),
            pl.BlockSpec((None, q_rank, hq), lambda i: (layer, 0, 0)),
            pl.BlockSpec((None, kv_rank, hq), lambda i: (layer, 0, 0)),
            tab_spec, tab_spec, tab_spec,
        ],
        out_specs=[
            pl.BlockSpec((tm, hq), lambda i: (i, 0)),
            pl.BlockSpec((tm, hq), lambda i: (i, 0)),
            pl.BlockSpec((tm, hv), lambda i: (i, 0)),
        ],
        out_shape=[
            jax.ShapeDtypeStruct((m, hq), BF16),
            jax.ShapeDtypeStruct((m, hq), BF16),
            jax.ShapeDtypeStruct((m, hv), BF16),
        ],
        compiler_params=pltpu.CompilerParams(
            dimension_semantics=("arbitrary",), vmem_limit_bytes=_vmem_limit(est)),
        name="mla_up",
    )(z, z, z, q_gains.reshape(depth, 1, q_rank), kv_gains.reshape(depth, 1, kv_rank), wq, wkv,
      c_tab, s_lo, s_hi)


def _attn_body(q_ref, k_ref, v_ref, g_ref, o_ref, vx_scr, bias_scr, *, seq, tq, hps):
    @pl.when((pl.program_id(0) == 0) & (pl.program_id(1) == 0))
    def _():
        row = lax.broadcasted_iota(jnp.int32, (tq, tq), 0)
        col = lax.broadcasted_iota(jnp.int32, (tq, tq), 1)
        bias_scr[...] = jnp.where(col <= row, 0.0, -jnp.inf).astype(F32)
        vx_scr[:, :, V_HEAD_DIM:] = jnp.ones((hps, seq, V_HEAD_DIM), BF16)

    nt = (((1,), (1,)), ((), ()))
    for hh in range(hps):
        vx_scr[hh, :, :V_HEAD_DIM] = v_ref[:, hh * V_HEAD_DIM:(hh + 1) * V_HEAD_DIM]
    for hh in range(hps):
        gain = g_ref[hh]
        qk_lanes = slice(hh * QK_PAD_DIM, (hh + 1) * QK_PAD_DIM)
        for qi in range(seq // tq):
            q0, q1 = qi * tq, (qi + 1) * tq
            q = q_ref[q0:q1, qk_lanes]
            s = lax.dot_general(q, k_ref[q0:q1, qk_lanes], nt, preferred_element_type=F32) + bias_scr[...]
            m = jnp.max(s, axis=-1, keepdims=True)
            acc = jnp.dot(jnp.exp2(s - m).astype(BF16), vx_scr[hh, q0:q1, :], preferred_element_type=F32)
            for kj in range(qi):
                k0, k1 = kj * tq, (kj + 1) * tq
                s = lax.dot_general(q, k_ref[k0:k1, qk_lanes], nt, preferred_element_type=F32)
                m_new = jnp.maximum(m, jnp.max(s, axis=-1, keepdims=True))
                acc = acc * jnp.exp2(m - m_new) + jnp.dot(
                    jnp.exp2(s - m_new).astype(BF16), vx_scr[hh, k0:k1, :], preferred_element_type=F32)
                m = m_new
            o = acc[:, :V_HEAD_DIM] / acc[:, V_HEAD_DIM:]
            o_ref[q0:q1, hh * V_HEAD_DIM:(hh + 1) * V_HEAD_DIM] = _rms(o, gain).astype(BF16)


def _attention(q, k, v, gains, *, layer, batch, seq, heads, tq=512, hps=4):
    tq = min(tq, seq)
    n_groups = gains.shape[1]
    assert seq % tq == 0 and heads % hps == 0 and n_groups % hps == 0
    gain_block0 = layer * (n_groups // hps)
    q3 = q.reshape(batch, seq, heads * QK_PAD_DIM)
    k3 = k.reshape(batch, seq, heads * QK_PAD_DIM)
    v3 = v.reshape(batch, seq, heads * V_HEAD_DIM)
    est = hps * (2 * (2 * seq * QK_PAD_DIM * 2 + 2 * seq * V_HEAD_DIM * 2) + seq * 2 * V_HEAD_DIM * 2
                 + 6 * tq * tq * 4) + tq * tq * 4
    return pl.pallas_call(
        functools.partial(_attn_body, seq=seq, tq=tq, hps=hps),
        grid=(batch, heads // hps),
        in_specs=[
            pl.BlockSpec((None, seq, hps * QK_PAD_DIM), lambda b, h: (b, 0, h)),
            pl.BlockSpec((None, seq, hps * QK_PAD_DIM), lambda b, h: (b, 0, h)),
            pl.BlockSpec((None, seq, hps * V_HEAD_DIM), lambda b, h: (b, 0, h)),
            pl.BlockSpec((hps, 1, GROUP_DIM), lambda b, h: (gain_block0 + h, 0, 0)),
        ],
        out_specs=pl.BlockSpec((None, seq, hps * V_HEAD_DIM), lambda b, h: (b, 0, h)),
        out_shape=jax.ShapeDtypeStruct((batch, seq, heads * V_HEAD_DIM), BF16),
        scratch_shapes=[pltpu.VMEM((hps, seq, 2 * V_HEAD_DIM), BF16), pltpu.VMEM((tq, tq), F32)],
        compiler_params=pltpu.CompilerParams(
            dimension_semantics=("arbitrary", "arbitrary"), vmem_limit_bytes=_vmem_limit(est)),
        name="attention",
    )(q3, k3, v3, gains.reshape(-1, 1, GROUP_DIM))


def _conv_body(a_ref, g_ref, sb_ref, scg_ref, sx_ref, cw_ref, cb_ref, lg_ref, lb_ref, sw_ref, og_ref,
               o_ref, u_scr, m_scr, y_scr, *, seq, rows, width):
    groups = width // LANES
    n_chunks = seq // rows

    u_scr[0:CONF_PAD_ROWS, :] = jnp.zeros((CONF_PAD_ROWS, width), F32)
    m_scr[0:SC_PAD_ROWS, :] = jnp.zeros((SC_PAD_ROWS, width), F32)

    def fill(c, carry):
        t0 = pl.multiple_of(c * rows, rows)
        a = a_ref[pl.ds(t0, rows), :].astype(F32)
        g = g_ref[pl.ds(t0, rows), :].astype(F32)
        u_scr[pl.ds(CONF_PAD_ROWS + t0, rows), :] = a * jax.nn.sigmoid(g)
        m_scr[pl.ds(SC_PAD_ROWS + t0, rows), :] = (
            scg_ref[pl.ds(t0, rows), :].astype(F32) * sx_ref[pl.ds(t0, rows), :].astype(F32))
        return carry

    lax.fori_loop(0, n_chunks, fill, 0)

    def causal_taps(src_ref, w_ref, t0, lanes, pad_rows, taps, acc):
        win_rows = rows + pad_rows
        win = src_ref[pl.ds(t0, win_rows), lanes]
        for r in range(SUBLANES):
            shifted = None
            for j in range(taps):
                off = pad_rows - (taps - 1) + j
                if off % SUBLANES != r:
                    continue
                if shifted is None:
                    shifted = win if r == 0 else pltpu.roll(win, win_rows - r, 0)
                base = off - r
                acc = acc + w_ref[j:j + 1, lanes] * shifted[base:base + rows]
        return acc

    def chunk(c, carry):
        t0 = pl.multiple_of(c * rows, rows)
        for cg in range(groups):
            lanes = slice(cg * LANES, (cg + 1) * LANES)
            acc = jnp.broadcast_to(cb_ref[:, lanes], (rows, LANES))
            y_scr[:, lanes] = causal_taps(u_scr, cw_ref, t0, lanes, CONF_PAD_ROWS, CONF_KERNEL, acc)
        y = y_scr[...]
        mu = jnp.mean(y, axis=-1, keepdims=True)
        dlt = y - mu
        var = jnp.mean(dlt * dlt, axis=-1, keepdims=True)
        ln = dlt * lax.rsqrt(var + LN_EPS) * lg_ref[...] + lb_ref[...]
        conf = ln * jax.nn.sigmoid(ln)
        for cg in range(groups):
            lanes = slice(cg * LANES, (cg + 1) * LANES)
            o_ref[pl.ds(t0, rows), lanes] = _rms(conf[:, lanes], og_ref[cg:cg + 1, :]).astype(BF16)
        for cg in range(groups):
            lanes = slice(cg * LANES, (cg + 1) * LANES)
            acc = causal_taps(m_scr, sw_ref, t0, lanes, SC_PAD_ROWS, SC_KERNEL,
                              jnp.zeros((rows, LANES), F32))
            sc = sb_ref[pl.ds(t0, rows), lanes].astype(F32) * acc
            o_ref[pl.ds(t0, rows), width + cg * LANES:width + (cg + 1) * LANES] = _rms(
                sc, og_ref[groups + cg:groups + cg + 1, :]).astype(BF16)
        return carry

    lax.fori_loop(0, n_chunks, chunk, 0)


def _conv_mixers(z, conf_w, conf_b, ln_g, ln_b, sc_w, gains, *, layer, batch, seq, width, first_block,
                 rows=128):
    nw = z.shape[1]
    depth = conf_w.shape[0]
    z3 = z.reshape(batch, seq, nw)
    rows = min(rows, seq)
    groups = width // LANES
    assert seq % rows == 0 and conf_w.shape[1:] == (CONF_KERNEL, width) and sc_w.shape[1:] == (SC_KERNEL, width)
    assert gains.shape[1] % (2 * groups) == 0
    gain_block = gains.shape[1] // (2 * groups) - 1

    def zspec(j):
        return pl.BlockSpec((None, seq, width), lambda b: (b, 0, first_block + j))

    def full(shape):
        return pl.BlockSpec((None,) + shape, lambda b: (layer,) + (0,) * len(shape))

    est = (2 * 5 * seq * width * 2 + 2 * seq * 2 * width * 2
           + (2 * seq + CONF_PAD_ROWS + SC_PAD_ROWS + rows) * width * 4)
    return pl.pallas_call(
        functools.partial(_conv_body, seq=seq, rows=rows, width=width),
        grid=(batch,),
        in_specs=[zspec(0), zspec(1), zspec(2), zspec(3), zspec(4),
                  full((CONF_KERNEL, width)), full((1, width)), full((1, width)), full((1, width)),
                  full((SC_KERNEL, width)),
                  pl.BlockSpec((None, 2 * groups, GROUP_DIM), lambda b: (layer, gain_block, 0))],
        out_specs=pl.BlockSpec((None, seq, 2 * width), lambda b: (b, 0, 0)),
        out_shape=jax.ShapeDtypeStruct((batch, seq, 2 * width), BF16),
        scratch_shapes=[pltpu.VMEM((CONF_PAD_ROWS + seq, width), F32),
                        pltpu.VMEM((SC_PAD_ROWS + seq, width), F32),
                        pltpu.VMEM((rows, width), F32)],
        compiler_params=pltpu.CompilerParams(
            dimension_semantics=("arbitrary",), vmem_limit_bytes=_vmem_limit(est)),
        name="conv_mixers",
    )(z3, z3, z3, z3, z3, conf_w, conf_b.reshape(depth, 1, width), ln_g.reshape(depth, 1, width),
      ln_b.reshape(depth, 1, width), sc_w, gains)


def _outproj_body(ya_ref, yc_ref, wa_ref, wc_ref, h_ref, o_ref):
    o_ref[...] = (h_ref[...]
                  + jnp.dot(ya_ref[...], wa_ref[...], preferred_element_type=F32)
                  + jnp.dot(yc_ref[...], wc_ref[...], preferred_element_type=F32))


def _outproj(y_attn, y_conv, w_o, h, *, layer, tm=512):
    m, d = h.shape
    ka, kc = y_attn.shape[1], y_conv.shape[1]
    tm = min(tm, m)
    assert m % tm == 0 and ka == kc and ka + kc == w_o.shape[1]
    est = 2 * tm * (ka + kc) * 2 + 2 * (ka + kc) * d * 2 + 4 * tm * d * 4 + tm * d * 4
    return pl.pallas_call(
        _outproj_body,
        grid=(m // tm,),
        in_specs=[
            pl.BlockSpec((tm, ka), lambda i: (i, 0)),
            pl.BlockSpec((tm, kc), lambda i: (i, 0)),
            pl.BlockSpec((None, ka, d), lambda i: (layer, 0, 0)),
            pl.BlockSpec((None, kc, d), lambda i: (layer, 1, 0)),
            pl.BlockSpec((tm, d), lambda i: (i, 0)),
        ],
        out_specs=pl.BlockSpec((tm, d), lambda i: (i, 0)),
        out_shape=jax.ShapeDtypeStruct((m, d), F32),
        compiler_params=pltpu.CompilerParams(
            dimension_semantics=("arbitrary",), vmem_limit_bytes=_vmem_limit(est)),
        name="outproj",
    )(y_attn, y_conv, w_o, w_o, h)


def _rope_tables(seq):
    half = QK_ROPE_DIM // 2
    inv = 1.0 / (ROPE_THETA ** (jnp.arange(0, QK_ROPE_DIM, 2, dtype=F32) / QK_ROPE_DIM))
    ang = jnp.arange(seq, dtype=F32)[:, None] * inv[None, :]
    cos, sin = jnp.cos(ang), jnp.sin(ang)
    zero = jnp.zeros((seq, half), F32)
    c_tab = jnp.concatenate([cos, cos, zero, zero], axis=-1)
    s_lo = jnp.concatenate([-sin, zero, zero, zero], axis=-1)
    s_hi = jnp.concatenate([zero, sin, zero, zero], axis=-1)
    return c_tab, s_lo, s_hi


def kernel(x, ffn1_norm, ffn1_w_gate, ffn1_w_up, ffn1_w_down, mix_norm, w_in, q_norm, w_uq, kv_norm, w_ukv, conf_dw_w, conf_dw_b, conf_ln_g, conf_ln_b, sc_w, mix_out_norm, w_o, ffn2_norm, ffn2_w_gate, ffn2_w_up, ffn2_w_down, final_norm):
    batch, seq, d = x.shape
    depth = w_in.shape[0]
    q_rank = q_norm.shape[1]
    kv_rank = kv_norm.shape[1]
    conf_width = conf_dw_w.shape[2]
    sc_width = sc_w.shape[2]
    heads = w_ukv.shape[2] // (QK_NOPE_DIM + V_HEAD_DIM)
    assert w_uq.shape[2] == heads * (QK_NOPE_DIM + QK_ROPE_DIM)
    assert conf_width == sc_width == q_rank == kv_rank
    assert w_in.shape[2] == q_rank + kv_rank + QK_ROPE_DIM + 2 * conf_width + 3 * sc_width
    assert heads * V_HEAD_DIM + conf_width + sc_width == w_o.shape[1]

    c_tab, s_lo, s_hi = _rope_tables(seq)
    h = x.reshape(batch * seq, d)
    rope_at = q_rank + kv_rank
    w_in_p = jnp.concatenate(
        [w_in[:, :, :rope_at], w_in[:, :, rope_at + QK_ROPE_DIM:], w_in[:, :, rope_at:rope_at + QK_ROPE_DIM],
         jnp.zeros((depth, d, LANES - QK_ROPE_DIM), w_in.dtype)], axis=2).astype(BF16)
    wq = w_uq.reshape(depth, q_rank, heads, QK_NOPE_DIM + QK_ROPE_DIM)
    wq_p = jnp.pad(wq, ((0, 0), (0, 0), (0, 0), (0, QK_PAD_DIM - QK_NOPE_DIM - QK_ROPE_DIM))).reshape(
        depth, q_rank, heads * QK_PAD_DIM).astype(BF16)
    wkv = w_ukv.astype(BF16)
    w_o_b = w_o.astype(BF16)

    for l in range(depth):
        h = _ffn(h, ffn1_norm, ffn1_w_gate, ffn1_w_up, ffn1_w_down, final_norm, layer=l, final_norm=False)
        z = _inproj(h, mix_norm, w_in_p, layer=l)
        q, k, v = _mla_up(z, q_norm, kv_norm, wq_p, wkv, c_tab, s_lo, s_hi, layer=l,
                          seq=seq, heads=heads, q_rank=q_rank, kv_rank=kv_rank)
        y_attn = _attention(q, k, v, mix_out_norm, layer=l, batch=batch, seq=seq, heads=heads)
        y_conv = _conv_mixers(z, conf_dw_w, conf_dw_b, conf_ln_g, conf_ln_b, sc_w, mix_out_norm, layer=l,
                              batch=batch, seq=seq, width=conf_width,
                              first_block=(q_rank + kv_rank) // conf_width)
        h = _outproj(y_attn.reshape(batch * seq, -1), y_conv.reshape(batch * seq, -1), w_o_b, h, layer=l)
        h = _ffn(h, ffn2_norm, ffn2_w_gate, ffn2_w_up, ffn2_w_down, final_norm, layer=l,
                 final_norm=(l == depth - 1))
    return h.reshape(batch, seq, d)
```

```python
import functools

import jax
import jax.numpy as jnp
from jax import lax
from jax.experimental import pallas as pl
from jax.experimental.pallas import tpu as pltpu

F32 = jnp.float32
BF16 = jnp.bfloat16

LANES = 128
SUBLANES = 8
VMEM_PHYSICAL_BYTES = 64 * 1024 * 1024

GROUP_DIM = 128
QK_NOPE_DIM = 128
QK_ROPE_DIM = 64
V_HEAD_DIM = 128
QK_PAD_DIM = 256
ROPE_THETA = 10000.0
CONF_KERNEL = 31
SC_KERNEL = 3
RMS_EPS = 1e-6
LN_EPS = 1e-5
LOG2_E = 1.4426950408889634

CONF_PAD_ROWS = 32
SC_PAD_ROWS = 8


def _vmem_limit(estimate_bytes):
    return int(min(estimate_bytes * 5 // 4 + (4 << 20), VMEM_PHYSICAL_BYTES - (2 << 20)))


def _rms(x, gain):
    return x * lax.rsqrt(jnp.mean(x * x, axis=-1, keepdims=True) + RMS_EPS) * gain


def _ffn_step(x_ref, g_ref, load_weights, fg_ref, o_ref, n_ref, final_norm):
    f = pl.program_id(1)

    @pl.when(f == 0)
    def _():
        x = x_ref[...]
        n_ref[...] = _rms(x, g_ref[...]).astype(BF16)
        o_ref[...] = x

    wg, wu, wd = load_weights()
    n = n_ref[...]
    gate = jnp.dot(n, wg, preferred_element_type=F32)
    up = jnp.dot(n, wu, preferred_element_type=F32)
    act = (gate * jax.nn.sigmoid(gate) * up * 0.5).astype(BF16)
    o_ref[...] += jnp.dot(act, wd, preferred_element_type=F32)

    if final_norm:
        @pl.when(f == pl.num_programs(1) - 1)
        def _():
            o_ref[...] = _rms(o_ref[...], fg_ref[...])


def _ffn_first_body(x_ref, g_ref, wg_ref, wu_ref, wd_ref, fg_ref, o_ref, wgo_ref, wuo_ref, wdo_ref, n_ref,
                    *, final_norm):
    def load_weights():
        wg = wg_ref[...].astype(BF16)
        wu = wu_ref[...].astype(BF16)
        wd = wd_ref[...].astype(BF16)
        wgo_ref[...] = wg
        wuo_ref[...] = wu
        wdo_ref[...] = wd
        return wg, wu, wd

    _ffn_step(x_ref, g_ref, load_weights, fg_ref, o_ref, n_ref, final_norm)


def _ffn_rest_body(x_ref, g_ref, wg_ref, wu_ref, wd_ref, fg_ref, first_ref, o_ref, n_ref, *, final_norm):
    del first_ref
    _ffn_step(x_ref, g_ref, lambda: (wg_ref[...], wu_ref[...], wd_ref[...]), fg_ref, o_ref, n_ref,
              final_norm)


def _ffn(h, gains, w_gate, w_up, w_down, final_gain, *, layer, final_norm, tm=1024, tf=512, tf_first=256):
    m, d = h.shape
    ff = w_gate.shape[2]
    tm = min(tm, m)
    assert m % tm == 0 and ff % tf == 0 and ff % tf_first == 0
    name = "ffn_final" if final_norm else "ffn"
    vec = pl.BlockSpec((None, 1, d), lambda i, f: (layer, 0, 0))
    fvec = pl.BlockSpec((1, d), lambda i, f: (0, 0))
    gains3 = gains.reshape(gains.shape[0], 1, d)
    fg = final_gain.reshape(1, d)
    scratch = [pltpu.VMEM((tm, d), BF16)]
    sem = ("arbitrary", "arbitrary")

    est = (4 * tm * d * 4 + tm * d * 2 + 2 * 3 * d * tf_first * (4 + 2) + 3 * d * tf_first * 2
           + tm * tf_first * (4 + 4 + 2))
    first, wg, wu, wd = pl.pallas_call(
        functools.partial(_ffn_first_body, final_norm=final_norm),
        grid=(1, ff // tf_first),
        in_specs=[
            pl.BlockSpec((tm, d), lambda i, f: (0, 0)),
            vec,
            pl.BlockSpec((None, d, tf_first), lambda i, f: (layer, 0, f)),
            pl.BlockSpec((None, d, tf_first), lambda i, f: (layer, 0, f)),
            pl.BlockSpec((None, tf_first, d), lambda i, f: (layer, f, 0)),
            fvec,
        ],
        out_specs=[
            pl.BlockSpec((tm, d), lambda i, f: (0, 0)),
            pl.BlockSpec((d, tf_first), lambda i, f: (0, f)),
            pl.BlockSpec((d, tf_first), lambda i, f: (0, f)),
            pl.BlockSpec((tf_first, d), lambda i, f: (f, 0)),
        ],
        out_shape=[
            jax.ShapeDtypeStruct((m, d), F32),
            jax.ShapeDtypeStruct((d, ff), BF16),
            jax.ShapeDtypeStruct((d, ff), BF16),
            jax.ShapeDtypeStruct((ff, d), BF16),
        ],
        scratch_shapes=scratch,
        compiler_params=pltpu.CompilerParams(dimension_semantics=sem, vmem_limit_bytes=_vmem_limit(est)),
        name=name + "_first",
    )(h, gains3, w_gate, w_up, w_down, fg)
    if m == tm:
        return first

    est = 4 * tm * d * 4 + tm * d * 2 + 2 * 3 * d * tf * 2 + tm * tf * (4 + 4 + 2)
    return pl.pallas_call(
        functools.partial(_ffn_rest_body, final_norm=final_norm),
        grid=(m // tm - 1, ff // tf),
        in_specs=[
            pl.BlockSpec((tm, d), lambda i, f: (i + 1, 0)),
            vec,
            pl.BlockSpec((d, tf), lambda i, f: (0, f)),
            pl.BlockSpec((d, tf), lambda i, f: (0, f)),
            pl.BlockSpec((tf, d), lambda i, f: (f, 0)),
            fvec,
            pl.BlockSpec(memory_space=pl.ANY),
        ],
        out_specs=pl.BlockSpec((tm, d), lambda i, f: (i + 1, 0)),
        out_shape=jax.ShapeDtypeStruct((m, d), F32),
        input_output_aliases={6: 0},
        scratch_shapes=scratch,
        compiler_params=pltpu.CompilerParams(dimension_semantics=sem, vmem_limit_bytes=_vmem_limit(est)),
        name=name,
    )(h, gains3, wg, wu, wd, fg, first)


def _rope128(x, c_tab, s_lo, s_hi):
    half = QK_ROPE_DIM // 2
    partner_hi = pltpu.roll(x, LANES - half, 1)
    partner_lo = pltpu.roll(x, half, 1)
    return x * c_tab + partner_hi * s_lo + partner_lo * s_hi


def _mla_up_body(cq_ref, ckv_ref, kr_ref, qg_ref, kvg_ref, wq_ref, wkv_ref,
                 c_ref, slo_ref, shi_ref, q_ref, k_ref, v_ref, *, heads):
    c_tab, s_lo, s_hi = c_ref[...], slo_ref[...], shi_ref[...]

    scale = float(QK_NOPE_DIM + QK_ROPE_DIM) ** -0.5 * LOG2_E
    nq = _rms(cq_ref[...].astype(F32), qg_ref[...] * scale).astype(BF16)
    q = jnp.dot(nq, wq_ref[...], preferred_element_type=F32)
    nkv = _rms(ckv_ref[...].astype(F32), kvg_ref[...]).astype(BF16)
    kv = jnp.dot(nkv, wkv_ref[...], preferred_element_type=F32)
    k_rope = _rope128(kr_ref[...].astype(F32), c_tab, s_lo, s_hi).astype(BF16)

    for hd in range(heads):
        base = hd * QK_PAD_DIM
        q_ref[:, base:base + LANES] = q[:, base:base + LANES].astype(BF16)
        q_ref[:, base + LANES:base + 2 * LANES] = _rope128(
            q[:, base + LANES:base + 2 * LANES], c_tab, s_lo, s_hi).astype(BF16)
        k_ref[:, base:base + LANES] = kv[:, base:base + LANES].astype(BF16)
        k_ref[:, base + LANES:base + 2 * LANES] = k_rope
        v_ref[:, hd * LANES:(hd + 1) * LANES] = kv[:, base + LANES:base + 2 * LANES].astype(BF16)


def _mla_up(z, q_gains, kv_gains, wq, wkv, c_tab, s_lo, s_hi, *, layer, seq, heads, q_rank, kv_rank,
            tm=512):
    m, nw = z.shape
    depth = q_gains.shape[0]
    tm = min(tm, seq)
    assert m % tm == 0 and seq % tm == 0 and q_rank == kv_rank
    s_blocks = seq // tm
    kr_block = (nw - LANES) // LANES
    hq = heads * QK_PAD_DIM
    hv = heads * V_HEAD_DIM
    est = (2 * (2 * tm * q_rank * 2 + tm * LANES * 2) + 2 * 2 * q_rank * hq * 2
           + 2 * 3 * tm * LANES * 4 + 2 * tm * (2 * hq + hv) * 2 + 2 * tm * hq * 4)
    tab_spec = pl.BlockSpec((tm, LANES), lambda i: (i % s_blocks, 0))
    return pl.pallas_call(
        functools.partial(_mla_up_body, heads=heads),
        grid=(m // tm,),
        in_specs=[
            pl.BlockSpec((tm, q_rank), lambda i: (i, 0)),
            pl.BlockSpec((tm, kv_rank), lambda i: (i, 1)),
            pl.BlockSpec((tm, LANES), lambda i: (i, kr_block)),
            pl.BlockSpec((None, 1, q_rank), lambda i: (layer, 0, 0)),
            pl.BlockSpec((None, 1, kv_rank), lambda i: (layer, 0, 0)),
            pl.BlockSpec((None, q_rank, hq), lambda i: (layer, 0, 0)),
            pl.BlockSpec((None, kv_rank, hq), lambda i: (layer, 0, 0)),
            tab_spec, tab_spec, tab_spec,
        ],
        out_specs=[
            pl.BlockSpec((tm, hq), lambda i: (i, 0)),
            pl.BlockSpec((tm, hq), lambda i: (i, 0)),
            pl.BlockSpec((tm, hv), lambda i: (i, 0)),
        ],
        out_shape=[
            jax.ShapeDtypeStruct((m, hq), BF16),
            jax.ShapeDtypeStruct((m, hq), BF16),
            jax.ShapeDtypeStruct((m, hv), BF16),
        ],
        compiler_params=pltpu.CompilerParams(
            dimension_semantics=("arbitrary",), vmem_limit_bytes=_vmem_limit(est)),
        name="mla_up",
    )(z, z, z, q_gains.reshape(depth, 1, q_rank), kv_gains.reshape(depth, 1, kv_rank), wq, wkv,
      c_tab, s_lo, s_hi)


def _attn_body(q_ref, k_ref, v_ref, g_ref, o_ref, vx_scr, bias_scr, *, seq, tq, hps):
    @pl.when((pl.program_id(0) == 0) & (pl.program_id(1) == 0))
    def _():
        row = lax.broadcasted_iota(jnp.int32, (tq, tq), 0)
        col = lax.broadcasted_iota(jnp.int32, (tq, tq), 1)
        bias_scr[...] = jnp.where(col <= row, 0.0, -jnp.inf).astype(F32)
        vx_scr[:, :, V_HEAD_DIM:] = jnp.ones((hps, seq, V_HEAD_DIM), BF16)

    nt = (((1,), (1,)), ((), ()))
    for hh in range(hps):
        vx_scr[hh, :, :V_HEAD_DIM] = v_ref[:, hh * V_HEAD_DIM:(hh + 1) * V_HEAD_DIM]
    for hh in range(hps):
        gain = g_ref[hh]
        qk_lanes = slice(hh * QK_PAD_DIM, (hh + 1) * QK_PAD_DIM)
        for qi in range(seq // tq):
            q0, q1 = qi * tq, (qi + 1) * tq
            q = q_ref[q0:q1, qk_lanes]
            s = lax.dot_general(q, k_ref[q0:q1, qk_lanes], nt, preferred_element_type=F32) + bias_scr[...]
            m = jnp.max(s, axis=-1, keepdims=True)
            acc = jnp.dot(jnp.exp2(s - m).astype(BF16), vx_scr[hh, q0:q1, :], preferred_element_type=F32)
            for kj in range(qi):
                k0, k1 = kj * tq, (kj + 1) * tq
                s = lax.dot_general(q, k_ref[k0:k1, qk_lanes], nt, preferred_element_type=F32)
                m_new = jnp.maximum(m, jnp.max(s, axis=-1, keepdims=True))
                acc = acc * jnp.exp2(m - m_new) + jnp.dot(
                    jnp.exp2(s - m_new).astype(BF16), vx_scr[hh, k0:k1, :], preferred_element_type=F32)
                m = m_new
            o = acc[:, :V_HEAD_DIM] / acc[:, V_HEAD_DIM:]
            o_ref[q0:q1, hh * V_HEAD_DIM:(hh + 1) * V_HEAD_DIM] = _rms(o, gain).astype(BF16)


def _attention(q, k, v, gains, *, layer, batch, seq, heads, tq=512, hps=4):
    tq = min(tq, seq)
    n_groups = gains.shape[1]
    assert seq % tq == 0 and heads % hps == 0 and n_groups % hps == 0
    gain_block0 = layer * (n_groups // hps)
    q3 = q.reshape(batch, seq, heads * QK_PAD_DIM)
    k3 = k.reshape(batch, seq, heads * QK_PAD_DIM)
    v3 = v.reshape(batch, seq, heads * V_HEAD_DIM)
    est = hps * (2 * (2 * seq * QK_PAD_DIM * 2 + 2 * seq * V_HEAD_DIM * 2) + seq * 2 * V_HEAD_DIM * 2
                 + 6 * tq * tq * 4) + tq * tq * 4
    return pl.pallas_call(
        functools.partial(_attn_body, seq=seq, tq=tq, hps=hps),
        grid=(batch, heads // hps),
        in_specs=[
            pl.BlockSpec((None, seq, hps * QK_PAD_DIM), lambda b, h: (b, 0, h)),
            pl.BlockSpec((None, seq, hps * QK_PAD_DIM), lambda b, h: (b, 0, h)),
            pl.BlockSpec((None, seq, hps * V_HEAD_DIM), lambda b, h: (b, 0, h)),
            pl.BlockSpec((hps, 1, GROUP_DIM), lambda b, h: (gain_block0 + h, 0, 0)),
        ],
        out_specs=pl.BlockSpec((None, seq, hps * V_HEAD_DIM), lambda b, h: (b, 0, h)),
        out_shape=jax.ShapeDtypeStruct((batch, seq, heads * V_HEAD_DIM), BF16),
        scratch_shapes=[pltpu.VMEM((hps, seq, 2 * V_HEAD_DIM), BF16), pltpu.VMEM((tq, tq), F32)],
        compiler_params=pltpu.CompilerParams(
            dimension_semantics=("arbitrary", "arbitrary"), vmem_limit_bytes=_vmem_limit(est)),
        name="attention",
    )(q3, k3, v3, gains.reshape(-1, 1, GROUP_DIM))


def _inproj_conv_body(x_ref, g_ref, wc_ref, wm_ref, cw_ref, cb_ref, lg_ref, lb_ref, sw_ref, og_ref,
                      zm_ref, y_ref, u_scr, m_scr, y_scr, *, tm, rows, width, tiles_per_seq):
    groups = width // LANES

    @pl.when(pl.program_id(0) % tiles_per_seq == 0)
    def _():
        u_scr[0:CONF_PAD_ROWS, :] = jnp.zeros((CONF_PAD_ROWS, width), F32)
        m_scr[0:SC_PAD_ROWS, :] = jnp.zeros((SC_PAD_ROWS, width), F32)

    n = _rms(x_ref[...], g_ref[...]).astype(BF16)
    zc = jnp.dot(n, wc_ref[...], preferred_element_type=F32)
    u_scr[CONF_PAD_ROWS:CONF_PAD_ROWS + tm, :] = zc[:, 0:width] * jax.nn.sigmoid(zc[:, width:2 * width])
    sb = zc[:, 2 * width:3 * width]
    m_scr[SC_PAD_ROWS:SC_PAD_ROWS + tm, :] = zc[:, 3 * width:4 * width] * zc[:, 4 * width:5 * width]

    def causal_taps(src_ref, w_ref, t0, lanes, pad_rows, taps, acc):
        win_rows = rows + pad_rows
        win = src_ref[t0:t0 + win_rows, lanes]
        for r in range(SUBLANES):
            shifted = None
            for j in range(taps):
                off = pad_rows - (taps - 1) + j
                if off % SUBLANES != r:
                    continue
                if shifted is None:
                    shifted = win if r == 0 else pltpu.roll(win, win_rows - r, 0)
                base = off - r
                acc = acc + w_ref[j:j + 1, lanes] * shifted[base:base + rows]
        return acc

    for c in range(tm // rows):
        t0 = c * rows
        for cg in range(groups):
            lanes = slice(cg * LANES, (cg + 1) * LANES)
            acc = jnp.broadcast_to(cb_ref[:, lanes], (rows, LANES))
            y_scr[:, lanes] = causal_taps(u_scr, cw_ref, t0, lanes, CONF_PAD_ROWS, CONF_KERNEL, acc)
        y = y_scr[...]
        mu = jnp.mean(y, axis=-1, keepdims=True)
        dlt = y - mu
        var = jnp.mean(dlt * dlt, axis=-1, keepdims=True)
        ln = dlt * lax.rsqrt(var + LN_EPS) * lg_ref[...] + lb_ref[...]
        conf = ln * jax.nn.sigmoid(ln)
        for cg in range(groups):
            lanes = slice(cg * LANES, (cg + 1) * LANES)
            y_ref[t0:t0 + rows, lanes] = _rms(conf[:, lanes], og_ref[cg:cg + 1, :]).astype(BF16)
        for cg in range(groups):
            lanes = slice(cg * LANES, (cg + 1) * LANES)
            acc = causal_taps(m_scr, sw_ref, t0, lanes, SC_PAD_ROWS, SC_KERNEL,
                              jnp.zeros((rows, LANES), F32))
            sc = sb[t0:t0 + rows, lanes] * acc
            y_ref[t0:t0 + rows, width + cg * LANES:width + (cg + 1) * LANES] = _rms(
                sc, og_ref[groups + cg:groups + cg + 1, :]).astype(BF16)

    u_scr[0:CONF_PAD_ROWS, :] = u_scr[tm:tm + CONF_PAD_ROWS, :]
    m_scr[0:SC_PAD_ROWS, :] = m_scr[tm:tm + SC_PAD_ROWS, :]
    zm_ref[...] = jnp.dot(n, wm_ref[...], preferred_element_type=F32).astype(BF16)


def _inproj_conv(h, gains, w_conv, w_mla, conf_w, conf_b, ln_g, ln_b, sc_w, out_gains, *, layer, seq, width,
                 tm=512, rows=128):
    m, d = h.shape
    depth = conf_w.shape[0]
    nc, nm = w_conv.shape[2], w_mla.shape[2]
    tm = min(tm, seq)
    rows = min(rows, tm)
    groups = width // LANES
    assert m % tm == 0 and seq % tm == 0 and tm % rows == 0 and nc == 5 * width
    assert conf_w.shape[1:] == (CONF_KERNEL, width) and sc_w.shape[1:] == (SC_KERNEL, width)
    assert out_gains.shape[1] % (2 * groups) == 0
    gain_block = out_gains.shape[1] // (2 * groups) - 1

    def full(shape):
        return pl.BlockSpec((None,) + shape, lambda i: (layer,) + (0,) * len(shape))

    est = (2 * tm * d * 4 + 2 * d * (nc + nm) * 2 + 2 * tm * (nm + 2 * width) * 2 + tm * (nc + nm) * 4
           + tm * d * 2 + (2 * tm + CONF_PAD_ROWS + SC_PAD_ROWS + rows) * width * 4)
    return pl.pallas_call(
        functools.partial(_inproj_conv_body, tm=tm, rows=rows, width=width, tiles_per_seq=seq // tm),
        grid=(m // tm,),
        in_specs=[
            pl.BlockSpec((tm, d), lambda i: (i, 0)),
            full((1, d)), full((d, nc)), full((d, nm)),
            full((CONF_KERNEL, width)), full((1, width)), full((1, width)), full((1, width)),
            full((SC_KERNEL, width)),
            pl.BlockSpec((None, 2 * groups, GROUP_DIM), lambda i: (layer, gain_block, 0)),
        ],
        out_specs=[pl.BlockSpec((tm, nm), lambda i: (i, 0)), pl.BlockSpec((tm, 2 * width), lambda i: (i, 0))],
        out_shape=[jax.ShapeDtypeStruct((m, nm), BF16), jax.ShapeDtypeStruct((m, 2 * width), BF16)],
        scratch_shapes=[pltpu.VMEM((CONF_PAD_ROWS + tm, width), F32),
                        pltpu.VMEM((SC_PAD_ROWS + tm, width), F32),
                        pltpu.VMEM((rows, width), F32)],
        compiler_params=pltpu.CompilerParams(
            dimension_semantics=("arbitrary",), vmem_limit_bytes=_vmem_limit(est)),
        name="inproj_conv",
    )(h, gains.reshape(depth, 1, d), w_conv, w_mla, conf_w, conf_b.reshape(depth, 1, width),
      ln_g.reshape(depth, 1, width), ln_b.reshape(depth, 1, width), sc_w, out_gains)


def _outproj_body(ya_ref, yc_ref, wa_ref, wc_ref, h_ref, o_ref):
    o_ref[...] = (h_ref[...]
                  + jnp.dot(ya_ref[...], wa_ref[...], preferred_element_type=F32)
                  + jnp.dot(yc_ref[...], wc_ref[...], preferred_element_type=F32))


def _outproj(y_attn, y_conv, w_o, h, *, layer, tm=512):
    m, d = h.shape
    ka, kc = y_attn.shape[1], y_conv.shape[1]
    tm = min(tm, m)
    assert m % tm == 0 and ka == kc and ka + kc == w_o.shape[1]
    est = 2 * tm * (ka + kc) * 2 + 2 * (ka + kc) * d * 2 + 4 * tm * d * 4 + tm * d * 4
    return pl.pallas_call(
        _outproj_body,
        grid=(m // tm,),
        in_specs=[
            pl.BlockSpec((tm, ka), lambda i: (i, 0)),
            pl.BlockSpec((tm, kc), lambda i: (i, 0)),
            pl.BlockSpec((None, ka, d), lambda i: (layer, 0, 0)),
            pl.BlockSpec((None, kc, d), lambda i: (layer, 1, 0)),
            pl.BlockSpec((tm, d), lambda i: (i, 0)),
        ],
        out_specs=pl.BlockSpec((tm, d), lambda i: (i, 0)),
        out_shape=jax.ShapeDtypeStruct((m, d), F32),
        compiler_params=pltpu.CompilerParams(
            dimension_semantics=("arbitrary",), vmem_limit_bytes=_vmem_limit(est)),
        name="outproj",
    )(y_attn, y_conv, w_o, w_o, h)


def _rope_tables(seq):
    half = QK_ROPE_DIM // 2
    inv = 1.0 / (ROPE_THETA ** (jnp.arange(0, QK_ROPE_DIM, 2, dtype=F32) / QK_ROPE_DIM))
    ang = jnp.arange(seq, dtype=F32)[:, None] * inv[None, :]
    cos, sin = jnp.cos(ang), jnp.sin(ang)
    zero = jnp.zeros((seq, half), F32)
    c_tab = jnp.concatenate([cos, cos, zero, zero], axis=-1)
    s_lo = jnp.concatenate([-sin, zero, zero, zero], axis=-1)
    s_hi = jnp.concatenate([zero, sin, zero, zero], axis=-1)
    return c_tab, s_lo, s_hi


def kernel(x, ffn1_norm, ffn1_w_gate, ffn1_w_up, ffn1_w_down, mix_norm, w_in, q_norm, w_uq, kv_norm, w_ukv, conf_dw_w, conf_dw_b, conf_ln_g, conf_ln_b, sc_w, mix_out_norm, w_o, ffn2_norm, ffn2_w_gate, ffn2_w_up, ffn2_w_down, final_norm):
    batch, seq, d = x.shape
    depth = w_in.shape[0]
    q_rank = q_norm.shape[1]
    kv_rank = kv_norm.shape[1]
    conf_width = conf_dw_w.shape[2]
    sc_width = sc_w.shape[2]
    heads = w_ukv.shape[2] // (QK_NOPE_DIM + V_HEAD_DIM)
    assert w_uq.shape[2] == heads * (QK_NOPE_DIM + QK_ROPE_DIM)
    assert conf_width == sc_width == q_rank == kv_rank
    assert w_in.shape[2] == q_rank + kv_rank + QK_ROPE_DIM + 2 * conf_width + 3 * sc_width
    assert heads * V_HEAD_DIM + conf_width + sc_width == w_o.shape[1]

    c_tab, s_lo, s_hi = _rope_tables(seq)
    h = x.reshape(batch * seq, d)
    rope_at = q_rank + kv_rank
    rope_end = rope_at + QK_ROPE_DIM
    w_mla = jnp.pad(w_in[:, :, :rope_end], ((0, 0), (0, 0), (0, LANES - QK_ROPE_DIM))).astype(BF16)
    w_conv = w_in[:, :, rope_end:].astype(BF16)
    wq = w_uq.reshape(depth, q_rank, heads, QK_NOPE_DIM + QK_ROPE_DIM)
    wq_p = jnp.pad(wq, ((0, 0), (0, 0), (0, 0), (0, QK_PAD_DIM - QK_NOPE_DIM - QK_ROPE_DIM))).reshape(
        depth, q_rank, heads * QK_PAD_DIM).astype(BF16)
    wkv = w_ukv.astype(BF16)
    w_o_b = w_o.astype(BF16)

    for l in range(depth):
        h = _ffn(h, ffn1_norm, ffn1_w_gate, ffn1_w_up, ffn1_w_down, final_norm, layer=l, final_norm=False)
        z_mla, y_conv = _inproj_conv(h, mix_norm, w_conv, w_mla, conf_dw_w, conf_dw_b, conf_ln_g, conf_ln_b,
                                     sc_w, mix_out_norm, layer=l, seq=seq, width=conf_width)
        q, k, v = _mla_up(z_mla, q_norm, kv_norm, wq_p, wkv, c_tab, s_lo, s_hi, layer=l,
                          seq=seq, heads=heads, q_rank=q_rank, kv_rank=kv_rank)
        y_attn = _attention(q, k, v, mix_out_norm, layer=l, batch=batch, seq=seq, heads=heads)
        h = _outproj(y_attn.reshape(batch * seq, -1), y_conv, w_o_b, h, layer=l)
        h = _ffn(h, ffn2_norm, ffn2_w_gate, ffn2_w_up, ffn2_w_down, final_norm, layer=l,
                 final_norm=(l == depth - 1))
    return h.reshape(batch, seq, d)
```

```python
import functools

import jax
import jax.numpy as jnp
from jax import lax
from jax.experimental import pallas as pl
from jax.experimental.pallas import tpu as pltpu

F32 = jnp.float32
BF16 = jnp.bfloat16

LANES = 128
SUBLANES = 8
VMEM_PHYSICAL_BYTES = 64 * 1024 * 1024

GROUP_DIM = 128
QK_NOPE_DIM = 128
QK_ROPE_DIM = 64
V_HEAD_DIM = 128
QK_PAD_DIM = 256
ROPE_THETA = 10000.0
CONF_KERNEL = 31
SC_KERNEL = 3
RMS_EPS = 1e-6
LN_EPS = 1e-5
LOG2_E = 1.4426950408889634

CONF_PAD_ROWS = 32
SC_PAD_ROWS = 8


def _vmem_limit(estimate_bytes):
    return int(min(estimate_bytes * 5 // 4 + (4 << 20), VMEM_PHYSICAL_BYTES - (2 << 20)))


def _rms(x, gain):
    return x * lax.rsqrt(jnp.mean(x * x, axis=-1, keepdims=True) + RMS_EPS) * gain


def _ffn_step(x_ref, g_ref, load_weights, fg_ref, o_ref, n_ref, final_norm):
    f = pl.program_id(1)

    def half_swiglu(n):
        wg, wu, wd = load_weights()
        gate = jnp.dot(n, wg, preferred_element_type=F32)
        up = jnp.dot(n, wu, preferred_element_type=F32)
        act = (gate * jax.nn.sigmoid(gate) * up * 0.5).astype(BF16)
        return jnp.dot(act, wd, preferred_element_type=F32)

    @pl.when(f == 0)
    def _():
        x = x_ref[...]
        n = _rms(x, g_ref[...]).astype(BF16)
        n_ref[...] = n
        o_ref[...] = x + half_swiglu(n)

    @pl.when(f != 0)
    def _():
        o_ref[...] += half_swiglu(n_ref[...])

    if final_norm:
        @pl.when(f == pl.num_programs(1) - 1)
        def _():
            o_ref[...] = _rms(o_ref[...], fg_ref[...])


def _ffn_first_body(x_ref, g_ref, wg_ref, wu_ref, wd_ref, fg_ref, o_ref, wgo_ref, wuo_ref, wdo_ref, n_ref,
                    *, final_norm):
    def load_weights():
        wg = wg_ref[...].astype(BF16)
        wu = wu_ref[...].astype(BF16)
        wd = wd_ref[...].astype(BF16)
        wgo_ref[...] = wg
        wuo_ref[...] = wu
        wdo_ref[...] = wd
        return wg, wu, wd

    _ffn_step(x_ref, g_ref, load_weights, fg_ref, o_ref, n_ref, final_norm)


def _ffn_rest_body(x_ref, g_ref, wg_ref, wu_ref, wd_ref, fg_ref, first_ref, o_ref, n_ref, *, final_norm):
    del first_ref
    _ffn_step(x_ref, g_ref, lambda: (wg_ref[...], wu_ref[...], wd_ref[...]), fg_ref, o_ref, n_ref,
              final_norm)


def _ffn(h, gains, w_gate, w_up, w_down, final_gain, *, layer, final_norm, tm=1024, tf=512, tf_first=256):
    m, d = h.shape
    ff = w_gate.shape[2]
    tm = min(tm, m)
    assert m % tm == 0 and ff % tf == 0 and ff % tf_first == 0
    name = "ffn_final" if final_norm else "ffn"
    vec = pl.BlockSpec((None, 1, d), lambda i, f: (layer, 0, 0))
    fvec = pl.BlockSpec((1, d), lambda i, f: (0, 0))
    gains3 = gains.reshape(gains.shape[0], 1, d)
    fg = final_gain.reshape(1, d)
    scratch = [pltpu.VMEM((tm, d), BF16)]
    sem = ("arbitrary", "arbitrary")

    est = (4 * tm * d * 4 + tm * d * 2 + 2 * 3 * d * tf_first * (4 + 2) + 3 * d * tf_first * 2
           + tm * tf_first * (4 + 4 + 2))
    first, wg, wu, wd = pl.pallas_call(
        functools.partial(_ffn_first_body, final_norm=final_norm),
        grid=(1, ff // tf_first),
        in_specs=[
            pl.BlockSpec((tm, d), lambda i, f: (0, 0)),
            vec,
            pl.BlockSpec((None, d, tf_first), lambda i, f: (layer, 0, f)),
            pl.BlockSpec((None, d, tf_first), lambda i, f: (layer, 0, f)),
            pl.BlockSpec((None, tf_first, d), lambda i, f: (layer, f, 0)),
            fvec,
        ],
        out_specs=[
            pl.BlockSpec((tm, d), lambda i, f: (0, 0)),
            pl.BlockSpec((d, tf_first), lambda i, f: (0, f)),
            pl.BlockSpec((d, tf_first), lambda i, f: (0, f)),
            pl.BlockSpec((tf_first, d), lambda i, f: (f, 0)),
        ],
        out_shape=[
            jax.ShapeDtypeStruct((m, d), F32),
            jax.ShapeDtypeStruct((d, ff), BF16),
            jax.ShapeDtypeStruct((d, ff), BF16),
            jax.ShapeDtypeStruct((ff, d), BF16),
        ],
        scratch_shapes=scratch,
        compiler_params=pltpu.CompilerParams(dimension_semantics=sem, vmem_limit_bytes=_vmem_limit(est)),
        name=name + "_first",
    )(h, gains3, w_gate, w_up, w_down, fg)
    if m == tm:
        return first

    est = 4 * tm * d * 4 + tm * d * 2 + 2 * 3 * d * tf * 2 + tm * tf * (4 + 4 + 2)
    return pl.pallas_call(
        functools.partial(_ffn_rest_body, final_norm=final_norm),
        grid=(m // tm - 1, ff // tf),
        in_specs=[
            pl.BlockSpec((tm, d), lambda i, f: (i + 1, 0)),
            vec,
            pl.BlockSpec((d, tf), lambda i, f: (0, f)),
            pl.BlockSpec((d, tf), lambda i, f: (0, f)),
            pl.BlockSpec((tf, d), lambda i, f: (f, 0)),
            fvec,
            pl.BlockSpec(memory_space=pl.ANY),
        ],
        out_specs=pl.BlockSpec((tm, d), lambda i, f: (i + 1, 0)),
        out_shape=jax.ShapeDtypeStruct((m, d), F32),
        input_output_aliases={6: 0},
        scratch_shapes=scratch,
        compiler_params=pltpu.CompilerParams(dimension_semantics=sem, vmem_limit_bytes=_vmem_limit(est)),
        name=name,
    )(h, gains3, wg, wu, wd, fg, first)


def _rope128(x, c_tab, s_lo, s_hi):
    half = QK_ROPE_DIM // 2
    partner_hi = pltpu.roll(x, LANES - half, 1)
    partner_lo = pltpu.roll(x, half, 1)
    return x * c_tab + partner_hi * s_lo + partner_lo * s_hi


def _mla_up_body(cq_ref, ckv_ref, kr_ref, qg_ref, kvg_ref, wq_ref, wkv_ref,
                 c_ref, slo_ref, shi_ref, q_ref, k_ref, v_ref, *, heads):
    c_tab, s_lo, s_hi = c_ref[...], slo_ref[...], shi_ref[...]

    scale = float(QK_NOPE_DIM + QK_ROPE_DIM) ** -0.5 * LOG2_E
    nq = _rms(cq_ref[...].astype(F32), qg_ref[...] * scale).astype(BF16)
    q = jnp.dot(nq, wq_ref[...], preferred_element_type=F32)
    nkv = _rms(ckv_ref[...].astype(F32), kvg_ref[...]).astype(BF16)
    kv = jnp.dot(nkv, wkv_ref[...], preferred_element_type=F32)
    k_rope = _rope128(kr_ref[...].astype(F32), c_tab, s_lo, s_hi).astype(BF16)

    for hd in range(heads):
        base = hd * QK_PAD_DIM
        q_ref[:, base:base + LANES] = q[:, base:base + LANES].astype(BF16)
        q_ref[:, base + LANES:base + 2 * LANES] = _rope128(
            q[:, base + LANES:base + 2 * LANES], c_tab, s_lo, s_hi).astype(BF16)
        k_ref[:, base:base + LANES] = kv[:, base:base + LANES].astype(BF16)
        k_ref[:, base + LANES:base + 2 * LANES] = k_rope
        v_ref[:, hd * LANES:(hd + 1) * LANES] = kv[:, base + LANES:base + 2 * LANES].astype(BF16)


def _mla_up(z, q_gains, kv_gains, wq, wkv, c_tab, s_lo, s_hi, *, layer, seq, heads, q_rank, kv_rank,
            tm=1024):
    m, nw = z.shape
    depth = q_gains.shape[0]
    tm = min(tm, seq)
    assert m % tm == 0 and seq % tm == 0 and q_rank == kv_rank
    s_blocks = seq // tm
    kr_block = (nw - LANES) // LANES
    hq = heads * QK_PAD_DIM
    hv = heads * V_HEAD_DIM
    est = (2 * (2 * tm * q_rank * 2 + tm * LANES * 2) + 2 * 2 * q_rank * hq * 2
           + 2 * 3 * tm * LANES * 4 + 2 * tm * (2 * hq + hv) * 2 + 2 * tm * hq * 4)
    tab_spec = pl.BlockSpec((tm, LANES), lambda i: (i % s_blocks, 0))
    return pl.pallas_call(
        functools.partial(_mla_up_body, heads=heads),
        grid=(m // tm,),
        in_specs=[
            pl.BlockSpec((tm, q_rank), lambda i: (i, 0)),
            pl.BlockSpec((tm, kv_rank), lambda i: (i, 1)),
            pl.BlockSpec((tm, LANES), lambda i: (i, kr_block)),
            pl.BlockSpec((None, 1, q_rank), lambda i: (layer, 0, 0)),
            pl.BlockSpec((None, 1, kv_rank), lambda i: (layer, 0, 0)),
            pl.BlockSpec((None, q_rank, hq), lambda i: (layer, 0, 0)),
            pl.BlockSpec((None, kv_rank, hq), lambda i: (layer, 0, 0)),
            tab_spec, tab_spec, tab_spec,
        ],
        out_specs=[
            pl.BlockSpec((tm, hq), lambda i: (i, 0)),
            pl.BlockSpec((tm, hq), lambda i: (i, 0)),
            pl.BlockSpec((tm, hv), lambda i: (i, 0)),
        ],
        out_shape=[
            jax.ShapeDtypeStruct((m, hq), BF16),
            jax.ShapeDtypeStruct((m, hq), BF16),
            jax.ShapeDtypeStruct((m, hv), BF16),
        ],
        compiler_params=pltpu.CompilerParams(
            dimension_semantics=("arbitrary",), vmem_limit_bytes=_vmem_limit(est)),
        name="mla_up",
    )(z, z, z, q_gains.reshape(depth, 1, q_rank), kv_gains.reshape(depth, 1, kv_rank), wq, wkv,
      c_tab, s_lo, s_hi)


def _attn_body(q_ref, k_ref, v_ref, g_ref, o_ref, vx_scr, bias_scr, *, seq, tq, hps):
    @pl.when((pl.program_id(0) == 0) & (pl.program_id(1) == 0))
    def _():
        row = lax.broadcasted_iota(jnp.int32, (tq, tq), 0)
        col = lax.broadcasted_iota(jnp.int32, (tq, tq), 1)
        bias_scr[...] = jnp.where(col <= row, 0.0, -jnp.inf).astype(F32)
        vx_scr[:, :, V_HEAD_DIM:] = jnp.ones((hps, seq, V_HEAD_DIM), BF16)

    nt = (((1,), (1,)), ((), ()))
    for hh in range(hps):
        vx_scr[hh, :, :V_HEAD_DIM] = v_ref[:, hh * V_HEAD_DIM:(hh + 1) * V_HEAD_DIM]
    for hh in range(hps):
        gain = g_ref[hh]
        qk_lanes = slice(hh * QK_PAD_DIM, (hh + 1) * QK_PAD_DIM)
        for qi in range(seq // tq):
            q0, q1 = qi * tq, (qi + 1) * tq
            q = q_ref[q0:q1, qk_lanes]
            s = lax.dot_general(q, k_ref[q0:q1, qk_lanes], nt, preferred_element_type=F32) + bias_scr[...]
            m = jnp.max(s, axis=-1, keepdims=True)
            acc = jnp.dot(jnp.exp2(s - m).astype(BF16), vx_scr[hh, q0:q1, :], preferred_element_type=F32)
            for kj in range(qi):
                k0, k1 = kj * tq, (kj + 1) * tq
                s = lax.dot_general(q, k_ref[k0:k1, qk_lanes], nt, preferred_element_type=F32)
                m_new = jnp.maximum(m, jnp.max(s, axis=-1, keepdims=True))
                acc = acc * jnp.exp2(m - m_new) + jnp.dot(
                    jnp.exp2(s - m_new).astype(BF16), vx_scr[hh, k0:k1, :], preferred_element_type=F32)
                m = m_new
            o = acc[:, :V_HEAD_DIM] / acc[:, V_HEAD_DIM:]
            o_ref[q0:q1, hh * V_HEAD_DIM:(hh + 1) * V_HEAD_DIM] = _rms(o, gain).astype(BF16)


def _attention(q, k, v, gains, *, layer, batch, seq, heads, tq=512, hps=4):
    tq = min(tq, seq)
    n_groups = gains.shape[1]
    assert seq % tq == 0 and heads % hps == 0 and n_groups % hps == 0
    gain_block0 = layer * (n_groups // hps)
    q3 = q.reshape(batch, seq, heads * QK_PAD_DIM)
    k3 = k.reshape(batch, seq, heads * QK_PAD_DIM)
    v3 = v.reshape(batch, seq, heads * V_HEAD_DIM)
    est = hps * (2 * (2 * seq * QK_PAD_DIM * 2 + 2 * seq * V_HEAD_DIM * 2) + seq * 2 * V_HEAD_DIM * 2
                 + 6 * tq * tq * 4) + tq * tq * 4
    return pl.pallas_call(
        functools.partial(_attn_body, seq=seq, tq=tq, hps=hps),
        grid=(batch, heads // hps),
        in_specs=[
            pl.BlockSpec((None, seq, hps * QK_PAD_DIM), lambda b, h: (b, 0, h)),
            pl.BlockSpec((None, seq, hps * QK_PAD_DIM), lambda b, h: (b, 0, h)),
            pl.BlockSpec((None, seq, hps * V_HEAD_DIM), lambda b, h: (b, 0, h)),
            pl.BlockSpec((hps, 1, GROUP_DIM), lambda b, h: (gain_block0 + h, 0, 0)),
        ],
        out_specs=pl.BlockSpec((None, seq, hps * V_HEAD_DIM), lambda b, h: (b, 0, h)),
        out_shape=jax.ShapeDtypeStruct((batch, seq, heads * V_HEAD_DIM), BF16),
        scratch_shapes=[pltpu.VMEM((hps, seq, 2 * V_HEAD_DIM), BF16), pltpu.VMEM((tq, tq), F32)],
        compiler_params=pltpu.CompilerParams(
            dimension_semantics=("arbitrary", "arbitrary"), vmem_limit_bytes=_vmem_limit(est)),
        name="attention",
    )(q3, k3, v3, gains.reshape(-1, 1, GROUP_DIM))


def _inproj_conv_body(x_ref, g_ref, wc_ref, wm_ref, cw_ref, cb_ref, lg_ref, lb_ref, sw_ref, og_ref,
                      zm_ref, y_ref, u_scr, m_scr, y_scr, *, tm, rows, width, tiles_per_seq):
    groups = width // LANES

    @pl.when(pl.program_id(0) % tiles_per_seq == 0)
    def _():
        u_scr[0:CONF_PAD_ROWS, :] = jnp.zeros((CONF_PAD_ROWS, width), F32)
        m_scr[0:SC_PAD_ROWS, :] = jnp.zeros((SC_PAD_ROWS, width), F32)

    n = _rms(x_ref[...], g_ref[...]).astype(BF16)
    zc = jnp.dot(n, wc_ref[...], preferred_element_type=F32)
    u_scr[CONF_PAD_ROWS:CONF_PAD_ROWS + tm, :] = zc[:, 0:width] * jax.nn.sigmoid(zc[:, width:2 * width])
    sb = zc[:, 2 * width:3 * width]
    m_scr[SC_PAD_ROWS:SC_PAD_ROWS + tm, :] = zc[:, 3 * width:4 * width] * zc[:, 4 * width:5 * width]

    def causal_taps(src_ref, w_ref, t0, lanes, pad_rows, taps, acc):
        win_rows = rows + pad_rows
        win = src_ref[t0:t0 + win_rows, lanes]
        for r in range(SUBLANES):
            shifted = None
            for j in range(taps):
                off = pad_rows - (taps - 1) + j
                if off % SUBLANES != r:
                    continue
                if shifted is None:
                    shifted = win if r == 0 else pltpu.roll(win, win_rows - r, 0)
                base = off - r
                acc = acc + w_ref[j:j + 1, lanes] * shifted[base:base + rows]
        return acc

    for c in range(tm // rows):
        t0 = c * rows
        for cg in range(groups):
            lanes = slice(cg * LANES, (cg + 1) * LANES)
            acc = jnp.broadcast_to(cb_ref[:, lanes], (rows, LANES))
            y_scr[:, lanes] = causal_taps(u_scr, cw_ref, t0, lanes, CONF_PAD_ROWS, CONF_KERNEL, acc)
        y = y_scr[...]
        mu = jnp.mean(y, axis=-1, keepdims=True)
        dlt = y - mu
        var = jnp.mean(dlt * dlt, axis=-1, keepdims=True)
        ln = dlt * lax.rsqrt(var + LN_EPS) * lg_ref[...] + lb_ref[...]
        conf = ln * jax.nn.sigmoid(ln)
        for cg in range(groups):
            lanes = slice(cg * LANES, (cg + 1) * LANES)
            y_ref[t0:t0 + rows, lanes] = _rms(conf[:, lanes], og_ref[cg:cg + 1, :]).astype(BF16)
        for cg in range(groups):
            lanes = slice(cg * LANES, (cg + 1) * LANES)
            acc = causal_taps(m_scr, sw_ref, t0, lanes, SC_PAD_ROWS, SC_KERNEL,
                              jnp.zeros((rows, LANES), F32))
            sc = sb[t0:t0 + rows, lanes] * acc
            y_ref[t0:t0 + rows, width + cg * LANES:width + (cg + 1) * LANES] = _rms(
                sc, og_ref[groups + cg:groups + cg + 1, :]).astype(BF16)

    u_scr[0:CONF_PAD_ROWS, :] = u_scr[tm:tm + CONF_PAD_ROWS, :]
    m_scr[0:SC_PAD_ROWS, :] = m_scr[tm:tm + SC_PAD_ROWS, :]
    zm_ref[...] = jnp.dot(n, wm_ref[...], preferred_element_type=F32).astype(BF16)


def _inproj_conv(h, gains, w_conv, w_mla, conf_w, conf_b, ln_g, ln_b, sc_w, out_gains, *, layer, seq, width,
                 tm=512, rows=128):
    m, d = h.shape
    depth = conf_w.shape[0]
    nc, nm = w_conv.shape[2], w_mla.shape[2]
    tm = min(tm, seq)
    rows = min(rows, tm)
    groups = width // LANES
    assert m % tm == 0 and seq % tm == 0 and tm % rows == 0 and nc == 5 * width
    assert conf_w.shape[1:] == (CONF_KERNEL, width) and sc_w.shape[1:] == (SC_KERNEL, width)
    assert out_gains.shape[1] % (2 * groups) == 0
    gain_block = out_gains.shape[1] // (2 * groups) - 1

    def full(shape):
        return pl.BlockSpec((None,) + shape, lambda i: (layer,) + (0,) * len(shape))

    est = (2 * tm * d * 4 + 2 * d * (nc + nm) * 2 + 2 * tm * (nm + 2 * width) * 2 + tm * (nc + nm) * 4
           + tm * d * 2 + (2 * tm + CONF_PAD_ROWS + SC_PAD_ROWS + rows) * width * 4)
    return pl.pallas_call(
        functools.partial(_inproj_conv_body, tm=tm, rows=rows, width=width, tiles_per_seq=seq // tm),
        grid=(m // tm,),
        in_specs=[
            pl.BlockSpec((tm, d), lambda i: (i, 0)),
            full((1, d)), full((d, nc)), full((d, nm)),
            full((CONF_KERNEL, width)), full((1, width)), full((1, width)), full((1, width)),
            full((SC_KERNEL, width)),
            pl.BlockSpec((None, 2 * groups, GROUP_DIM), lambda i: (layer, gain_block, 0)),
        ],
        out_specs=[pl.BlockSpec((tm, nm), lambda i: (i, 0)), pl.BlockSpec((tm, 2 * width), lambda i: (i, 0))],
        out_shape=[jax.ShapeDtypeStruct((m, nm), BF16), jax.ShapeDtypeStruct((m, 2 * width), BF16)],
        scratch_shapes=[pltpu.VMEM((CONF_PAD_ROWS + tm, width), F32),
                        pltpu.VMEM((SC_PAD_ROWS + tm, width), F32),
                        pltpu.VMEM((rows, width), F32)],
        compiler_params=pltpu.CompilerParams(
            dimension_semantics=("arbitrary",), vmem_limit_bytes=_vmem_limit(est)),
        name="inproj_conv",
    )(h, gains.reshape(depth, 1, d), w_conv, w_mla, conf_w, conf_b.reshape(depth, 1, width),
      ln_g.reshape(depth, 1, width), ln_b.reshape(depth, 1, width), sc_w, out_gains)


def _outproj_body(ya_ref, yc_ref, wa_ref, wc_ref, h_ref, o_ref):
    o_ref[...] = (h_ref[...]
                  + jnp.dot(ya_ref[...], wa_ref[...], preferred_element_type=F32)
                  + jnp.dot(yc_ref[...], wc_ref[...], preferred_element_type=F32))


def _outproj(y_attn, y_conv, w_o, h, *, layer, tm=1024):
    m, d = h.shape
    ka, kc = y_attn.shape[1], y_conv.shape[1]
    tm = min(tm, m)
    assert m % tm == 0 and ka == kc and ka + kc == w_o.shape[1]
    est = 2 * tm * (ka + kc) * 2 + 2 * (ka + kc) * d * 2 + 4 * tm * d * 4 + tm * d * 4
    return pl.pallas_call(
        _outproj_body,
        grid=(m // tm,),
        in_specs=[
            pl.BlockSpec((tm, ka), lambda i: (i, 0)),
            pl.BlockSpec((tm, kc), lambda i: (i, 0)),
            pl.BlockSpec((None, ka, d), lambda i: (layer, 0, 0)),
            pl.BlockSpec((None, kc, d), lambda i: (layer, 1, 0)),
            pl.BlockSpec((tm, d), lambda i: (i, 0)),
        ],
        out_specs=pl.BlockSpec((tm, d), lambda i: (i, 0)),
        out_shape=jax.ShapeDtypeStruct((m, d), F32),
        compiler_params=pltpu.CompilerParams(
            dimension_semantics=("arbitrary",), vmem_limit_bytes=_vmem_limit(est)),
        name="outproj",
    )(y_attn, y_conv, w_o, w_o, h)


def _rope_tables(seq):
    half = QK_ROPE_DIM // 2
    inv = 1.0 / (ROPE_THETA ** (jnp.arange(0, QK_ROPE_DIM, 2, dtype=F32) / QK_ROPE_DIM))
    ang = jnp.arange(seq, dtype=F32)[:, None] * inv[None, :]
    cos, sin = jnp.cos(ang), jnp.sin(ang)
    zero = jnp.zeros((seq, half), F32)
    c_tab = jnp.concatenate([cos, cos, zero, zero], axis=-1)
    s_lo = jnp.concatenate([-sin, zero, zero, zero], axis=-1)
    s_hi = jnp.concatenate([zero, sin, zero, zero], axis=-1)
    return c_tab, s_lo, s_hi


def kernel(x, ffn1_norm, ffn1_w_gate, ffn1_w_up, ffn1_w_down, mix_norm, w_in, q_norm, w_uq, kv_norm, w_ukv, conf_dw_w, conf_dw_b, conf_ln_g, conf_ln_b, sc_w, mix_out_norm, w_o, ffn2_norm, ffn2_w_gate, ffn2_w_up, ffn2_w_down, final_norm):
    batch, seq, d = x.shape
    depth = w_in.shape[0]
    q_rank = q_norm.shape[1]
    kv_rank = kv_norm.shape[1]
    conf_width = conf_dw_w.shape[2]
    sc_width = sc_w.shape[2]
    heads = w_ukv.shape[2] // (QK_NOPE_DIM + V_HEAD_DIM)
    assert w_uq.shape[2] == heads * (QK_NOPE_DIM + QK_ROPE_DIM)
    assert conf_width == sc_width == q_rank == kv_rank
    assert w_in.shape[2] == q_rank + kv_rank + QK_ROPE_DIM + 2 * conf_width + 3 * sc_width
    assert heads * V_HEAD_DIM + conf_width + sc_width == w_o.shape[1]

    c_tab, s_lo, s_hi = _rope_tables(seq)
    h = x.reshape(batch * seq, d)
    rope_at = q_rank + kv_rank
    rope_end = rope_at + QK_ROPE_DIM
    w_in_b = w_in.astype(BF16)
    w_mla = jnp.pad(w_in_b[:, :, :rope_end], ((0, 0), (0, 0), (0, LANES - QK_ROPE_DIM)))
    w_conv = w_in_b[:, :, rope_end:]
    wq = w_uq.reshape(depth, q_rank, heads, QK_NOPE_DIM + QK_ROPE_DIM)
    wq_p = jnp.pad(wq, ((0, 0), (0, 0), (0, 0), (0, QK_PAD_DIM - QK_NOPE_DIM - QK_ROPE_DIM))).reshape(
        depth, q_rank, heads * QK_PAD_DIM).astype(BF16)
    wkv = w_ukv.astype(BF16)
    w_o_b = w_o.astype(BF16)

    for l in range(depth):
        h = _ffn(h, ffn1_norm, ffn1_w_gate, ffn1_w_up, ffn1_w_down, final_norm, layer=l, final_norm=False)
        z_mla, y_conv = _inproj_conv(h, mix_norm, w_conv, w_mla, conf_dw_w, conf_dw_b, conf_ln_g, conf_ln_b,
                                     sc_w, mix_out_norm, layer=l, seq=seq, width=conf_width)
        q, k, v = _mla_up(z_mla, q_norm, kv_norm, wq_p, wkv, c_tab, s_lo, s_hi, layer=l,
                          seq=seq, heads=heads, q_rank=q_rank, kv_rank=kv_rank)
        y_attn = _attention(q, k, v, mix_out_norm, layer=l, batch=batch, seq=seq, heads=heads)
        h = _outproj(y_attn.reshape(batch * seq, -1), y_conv, w_o_b, h, layer=l)
        h = _ffn(h, ffn2_norm, ffn2_w_gate, ffn2_w_up, ffn2_w_down, final_norm, layer=l,
                 final_norm=(l == depth - 1))
    return h.reshape(batch, seq, d)
```

```python
import functools

import jax
import jax.numpy as jnp
from jax import lax
from jax.experimental import pallas as pl
from jax.experimental.pallas import tpu as pltpu

F32 = jnp.float32
BF16 = jnp.bfloat16

LANES = 128
SUBLANES = 8
VMEM_PHYSICAL_BYTES = 64 * 1024 * 1024

GROUP_DIM = 128
QK_NOPE_DIM = 128
QK_ROPE_DIM = 64
V_HEAD_DIM = 128
QK_PAD_DIM = 256
ROPE_THETA = 10000.0
CONF_KERNEL = 31
SC_KERNEL = 3
RMS_EPS = 1e-6
LN_EPS = 1e-5
LOG2_E = 1.4426950408889634

CONF_PAD_ROWS = 32
SC_PAD_ROWS = 8


def _vmem_limit(estimate_bytes):
    return int(min(estimate_bytes * 5 // 4 + (4 << 20), VMEM_PHYSICAL_BYTES - (2 << 20)))


def _rms(x, gain):
    return x * lax.rsqrt(jnp.mean(x * x, axis=-1, keepdims=True) + RMS_EPS) * gain


def _ffn_step(x_ref, g_ref, load_weights, fg_ref, o_ref, n_ref, final_norm):
    f = pl.program_id(1)

    def half_swiglu(n):
        wg, wu, wd = load_weights()
        gate = jnp.dot(n, wg, preferred_element_type=F32)
        up = jnp.dot(n, wu, preferred_element_type=F32)
        act = (gate * jax.nn.sigmoid(gate) * up * 0.5).astype(BF16)
        return jnp.dot(act, wd, preferred_element_type=F32)

    @pl.when(f == 0)
    def _():
        x = x_ref[...]
        n = _rms(x, g_ref[...]).astype(BF16)
        n_ref[...] = n
        o_ref[...] = x + half_swiglu(n)

    @pl.when(f != 0)
    def _():
        o_ref[...] += half_swiglu(n_ref[...])

    if final_norm:
        @pl.when(f == pl.num_programs(1) - 1)
        def _():
            o_ref[...] = _rms(o_ref[...], fg_ref[...])


def _ffn_first_body(x_ref, g_ref, wg_ref, wu_ref, wd_ref, fg_ref, o_ref, wgo_ref, wuo_ref, wdo_ref, n_ref,
                    *, final_norm):
    def load_weights():
        wg = wg_ref[...].astype(BF16)
        wu = wu_ref[...].astype(BF16)
        wd = wd_ref[...].astype(BF16)
        wgo_ref[...] = wg
        wuo_ref[...] = wu
        wdo_ref[...] = wd
        return wg, wu, wd

    _ffn_step(x_ref, g_ref, load_weights, fg_ref, o_ref, n_ref, final_norm)


def _ffn_rest_body(x_ref, g_ref, wg_ref, wu_ref, wd_ref, fg_ref, o_ref, n_ref, *, final_norm):
    _ffn_step(x_ref, g_ref, lambda: (wg_ref[...], wu_ref[...], wd_ref[...]), fg_ref, o_ref, n_ref,
              final_norm)


def _ffn_rest_copy_body(x_ref, g_ref, wg_ref, wu_ref, wd_ref, fg_ref, first_ref, o_ref, n_ref):
    i = pl.program_id(0)

    @pl.when((i == 0) & (pl.program_id(1) == 0))
    def _():
        pltpu.sync_copy(first_ref, o_ref)

    @pl.when(i > 0)
    def _():
        _ffn_step(x_ref, g_ref, lambda: (wg_ref[...], wu_ref[...], wd_ref[...]), fg_ref, o_ref, n_ref,
                  False)


def _ffn(h, gains, w_gate, w_up, w_down, final_gain, *, layer, final_norm, in_place,
         tm=1024, tf=512, tf_first=256):
    m, d = h.shape
    ff = w_gate.shape[2]
    tm = min(tm, m)
    assert m % tm == 0 and ff % tf == 0 and ff % tf_first == 0
    assert in_place or not final_norm
    name = "ffn_final" if final_norm else "ffn"
    vec = pl.BlockSpec((None, 1, d), lambda i, f: (layer, 0, 0))
    fvec = pl.BlockSpec((1, d), lambda i, f: (0, 0))
    gains3 = gains.reshape(gains.shape[0], 1, d)
    fg = final_gain.reshape(1, d)
    scratch = [pltpu.VMEM((tm, d), BF16)]
    sem = ("arbitrary", "arbitrary")

    est = (4 * tm * d * 4 + tm * d * 2 + 2 * 3 * d * tf_first * (4 + 2) + 3 * d * tf_first * 2
           + tm * tf_first * (4 + 4 + 2))
    first, wg, wu, wd = pl.pallas_call(
        functools.partial(_ffn_first_body, final_norm=final_norm),
        grid=(1, ff // tf_first),
        in_specs=[
            pl.BlockSpec((tm, d), lambda i, f: (0, 0)),
            vec,
            pl.BlockSpec((None, d, tf_first), lambda i, f: (layer, 0, f)),
            pl.BlockSpec((None, d, tf_first), lambda i, f: (layer, 0, f)),
            pl.BlockSpec((None, tf_first, d), lambda i, f: (layer, f, 0)),
            fvec,
        ],
        out_specs=[
            pl.BlockSpec((tm, d), lambda i, f: (0, 0)),
            pl.BlockSpec((d, tf_first), lambda i, f: (0, f)),
            pl.BlockSpec((d, tf_first), lambda i, f: (0, f)),
            pl.BlockSpec((tf_first, d), lambda i, f: (f, 0)),
        ],
        out_shape=[
            jax.ShapeDtypeStruct((m, d) if in_place else (tm, d), F32),
            jax.ShapeDtypeStruct((d, ff), BF16),
            jax.ShapeDtypeStruct((d, ff), BF16),
            jax.ShapeDtypeStruct((ff, d), BF16),
        ],
        input_output_aliases={0: 0} if in_place else {},
        scratch_shapes=scratch,
        compiler_params=pltpu.CompilerParams(dimension_semantics=sem, vmem_limit_bytes=_vmem_limit(est)),
        name=name + "_first",
    )(h, gains3, w_gate, w_up, w_down, fg)
    if m == tm:
        return first

    est = 4 * tm * d * 4 + tm * d * 2 + 2 * 3 * d * tf * 2 + tm * tf * (4 + 4 + 2)
    params = pltpu.CompilerParams(dimension_semantics=sem, vmem_limit_bytes=_vmem_limit(est))
    if in_place:
        return pl.pallas_call(
            functools.partial(_ffn_rest_body, final_norm=final_norm),
            grid=(m // tm - 1, ff // tf),
            in_specs=[
                pl.BlockSpec((tm, d), lambda i, f: (i + 1, 0)),
                vec,
                pl.BlockSpec((d, tf), lambda i, f: (0, f)),
                pl.BlockSpec((d, tf), lambda i, f: (0, f)),
                pl.BlockSpec((tf, d), lambda i, f: (f, 0)),
                fvec,
            ],
            out_specs=pl.BlockSpec((tm, d), lambda i, f: (i + 1, 0)),
            out_shape=jax.ShapeDtypeStruct((m, d), F32),
            input_output_aliases={0: 0},
            scratch_shapes=scratch,
            compiler_params=params,
            name=name,
        )(first, gains3, wg, wu, wd, fg)

    def hidden(i, f):
        return jnp.where(i == 0, 0, f)

    return pl.pallas_call(
        _ffn_rest_copy_body,
        grid=(m // tm, ff // tf),
        in_specs=[
            pl.BlockSpec((tm, d), lambda i, f: (jnp.maximum(i, 1), 0)),
            vec,
            pl.BlockSpec((d, tf), lambda i, f: (0, hidden(i, f))),
            pl.BlockSpec((d, tf), lambda i, f: (0, hidden(i, f))),
            pl.BlockSpec((tf, d), lambda i, f: (hidden(i, f), 0)),
            fvec,
            pl.BlockSpec(memory_space=pl.ANY),
        ],
        out_specs=pl.BlockSpec((tm, d), lambda i, f: (i, 0)),
        out_shape=jax.ShapeDtypeStruct((m, d), F32),
        scratch_shapes=scratch,
        compiler_params=params,
        name=name + "_assemble",
    )(h, gains3, wg, wu, wd, fg, first)


def _rope128(x, c_tab, s_lo, s_hi):
    half = QK_ROPE_DIM // 2
    partner_hi = pltpu.roll(x, LANES - half, 1)
    partner_lo = pltpu.roll(x, half, 1)
    return x * c_tab + partner_hi * s_lo + partner_lo * s_hi


def _mla_up_body(cq_ref, ckv_ref, kr_ref, qg_ref, kvg_ref, wq_ref, wkv_ref,
                 c_ref, slo_ref, shi_ref, q_ref, k_ref, v_ref, *, heads):
    c_tab, s_lo, s_hi = c_ref[...], slo_ref[...], shi_ref[...]

    scale = float(QK_NOPE_DIM + QK_ROPE_DIM) ** -0.5 * LOG2_E
    nq = _rms(cq_ref[...].astype(F32), qg_ref[...] * scale).astype(BF16)
    q = jnp.dot(nq, wq_ref[...], preferred_element_type=F32)
    nkv = _rms(ckv_ref[...].astype(F32), kvg_ref[...]).astype(BF16)
    kv = jnp.dot(nkv, wkv_ref[...], preferred_element_type=F32)
    k_rope = _rope128(kr_ref[...].astype(F32), c_tab, s_lo, s_hi).astype(BF16)

    for hd in range(heads):
        base = hd * QK_PAD_DIM
        q_ref[:, base:base + LANES] = q[:, base:base + LANES].astype(BF16)
        q_ref[:, base + LANES:base + 2 * LANES] = _rope128(
            q[:, base + LANES:base + 2 * LANES], c_tab, s_lo, s_hi).astype(BF16)
        k_ref[:, base:base + LANES] = kv[:, base:base + LANES].astype(BF16)
        k_ref[:, base + LANES:base + 2 * LANES] = k_rope
        v_ref[:, hd * LANES:(hd + 1) * LANES] = kv[:, base + LANES:base + 2 * LANES].astype(BF16)


def _mla_up(z, q_gains, kv_gains, wq, wkv, c_tab, s_lo, s_hi, *, layer, seq, heads, q_rank, kv_rank,
            tm=1024):
    m, nw = z.shape
    depth = q_gains.shape[0]
    tm = min(tm, seq)
    assert m % tm == 0 and seq % tm == 0 and q_rank == kv_rank
    s_blocks = seq // tm
    kr_block = (nw - LANES) // LANES
    hq = heads * QK_PAD_DIM
    hv = heads * V_HEAD_DIM
    est = (2 * (2 * tm * q_rank * 2 + tm * LANES * 2) + 2 * 2 * q_rank * hq * 2
           + 2 * 3 * tm * LANES * 4 + 2 * tm * (2 * hq + hv) * 2 + 2 * tm * hq * 4)
    tab_spec = pl.BlockSpec((tm, LANES), lambda i: (i % s_blocks, 0))
    return pl.pallas_call(
        functools.partial(_mla_up_body, heads=heads),
        grid=(m // tm,),
        in_specs=[
            pl.BlockSpec((tm, q_rank), lambda i: (i, 0)),
            pl.BlockSpec((tm, kv_rank), lambda i: (i, 1)),
            pl.BlockSpec((tm, LANES), lambda i: (i, kr_block)),
            pl.BlockSpec((None, 1, q_rank), lambda i: (layer, 0, 0)),
            pl.BlockSpec((None, 1, kv_rank), lambda i: (layer, 0, 0)),
            pl.BlockSpec((None, q_rank, hq), lambda i: (layer, 0, 0)),
            pl.BlockSpec((None, kv_rank, hq), lambda i: (layer, 0, 0)),
            tab_spec, tab_spec, tab_spec,
        ],
        out_specs=[
            pl.BlockSpec((tm, hq), lambda i: (i, 0)),
            pl.BlockSpec((tm, hq), lambda i: (i, 0)),
            pl.BlockSpec((tm, hv), lambda i: (i, 0)),
        ],
        out_shape=[
            jax.ShapeDtypeStruct((m, hq), BF16),
            jax.ShapeDtypeStruct((m, hq), BF16),
            jax.ShapeDtypeStruct((m, hv), BF16),
        ],
        compiler_params=pltpu.CompilerParams(
            dimension_semantics=("arbitrary",), vmem_limit_bytes=_vmem_limit(est)),
        name="mla_up",
    )(z, z, z, q_gains.reshape(depth, 1, q_rank), kv_gains.reshape(depth, 1, kv_rank), wq, wkv,
      c_tab, s_lo, s_hi)


def _attn_body(q_ref, k_ref, v_ref, g_ref, o_ref, vx_scr, bias_scr, *, seq, tq, hps):
    @pl.when((pl.program_id(0) == 0) & (pl.program_id(1) == 0))
    def _():
        row = lax.broadcasted_iota(jnp.int32, (tq, tq), 0)
        col = lax.broadcasted_iota(jnp.int32, (tq, tq), 1)
        bias_scr[...] = jnp.where(col <= row, 0.0, -jnp.inf).astype(F32)
        vx_scr[:, :, V_HEAD_DIM:] = jnp.ones((hps, seq, V_HEAD_DIM), BF16)

    nt = (((1,), (1,)), ((), ()))
    for hh in range(hps):
        vx_scr[hh, :, :V_HEAD_DIM] = v_ref[:, hh * V_HEAD_DIM:(hh + 1) * V_HEAD_DIM]
    for hh in range(hps):
        gain = g_ref[hh]
        qk_lanes = slice(hh * QK_PAD_DIM, (hh + 1) * QK_PAD_DIM)
        for qi in range(seq // tq):
            q0, q1 = qi * tq, (qi + 1) * tq
            q = q_ref[q0:q1, qk_lanes]
            s = lax.dot_general(q, k_ref[q0:q1, qk_lanes], nt, preferred_element_type=F32) + bias_scr[...]
            m = jnp.max(s, axis=-1, keepdims=True)
            acc = jnp.dot(jnp.exp2(s - m).astype(BF16), vx_scr[hh, q0:q1, :], preferred_element_type=F32)
            for kj in range(qi):
                k0, k1 = kj * tq, (kj + 1) * tq
                s = lax.dot_general(q, k_ref[k0:k1, qk_lanes], nt, preferred_element_type=F32)
                m_new = jnp.maximum(m, jnp.max(s, axis=-1, keepdims=True))
                acc = acc * jnp.exp2(m - m_new) + jnp.dot(
                    jnp.exp2(s - m_new).astype(BF16), vx_scr[hh, k0:k1, :], preferred_element_type=F32)
                m = m_new
            o = acc[:, :V_HEAD_DIM] / acc[:, V_HEAD_DIM:]
            o_ref[q0:q1, hh * V_HEAD_DIM:(hh + 1) * V_HEAD_DIM] = _rms(o, gain).astype(BF16)


def _attention(q, k, v, gains, *, layer, batch, seq, heads, tq=512, hps=4):
    tq = min(tq, seq)
    n_groups = gains.shape[1]
    assert seq % tq == 0 and heads % hps == 0 and n_groups % hps == 0
    gain_block0 = layer * (n_groups // hps)
    q3 = q.reshape(batch, seq, heads * QK_PAD_DIM)
    k3 = k.reshape(batch, seq, heads * QK_PAD_DIM)
    v3 = v.reshape(batch, seq, heads * V_HEAD_DIM)
    est = hps * (2 * (2 * seq * QK_PAD_DIM * 2 + 2 * seq * V_HEAD_DIM * 2) + seq * 2 * V_HEAD_DIM * 2
                 + 6 * tq * tq * 4) + tq * tq * 4
    return pl.pallas_call(
        functools.partial(_attn_body, seq=seq, tq=tq, hps=hps),
        grid=(batch, heads // hps),
        in_specs=[
            pl.BlockSpec((None, seq, hps * QK_PAD_DIM), lambda b, h: (b, 0, h)),
            pl.BlockSpec((None, seq, hps * QK_PAD_DIM), lambda b, h: (b, 0, h)),
            pl.BlockSpec((None, seq, hps * V_HEAD_DIM), lambda b, h: (b, 0, h)),
            pl.BlockSpec((hps, 1, GROUP_DIM), lambda b, h: (gain_block0 + h, 0, 0)),
        ],
        out_specs=pl.BlockSpec((None, seq, hps * V_HEAD_DIM), lambda b, h: (b, 0, h)),
        out_shape=jax.ShapeDtypeStruct((batch, seq, heads * V_HEAD_DIM), BF16),
        scratch_shapes=[pltpu.VMEM((hps, seq, 2 * V_HEAD_DIM), BF16), pltpu.VMEM((tq, tq), F32)],
        compiler_params=pltpu.CompilerParams(
            dimension_semantics=("arbitrary", "arbitrary"), vmem_limit_bytes=_vmem_limit(est)),
        name="attention",
    )(q3, k3, v3, gains.reshape(-1, 1, GROUP_DIM))


def _inproj_conv_body(x_ref, g_ref, wc_ref, wm_ref, cw_ref, cb_ref, lg_ref, lb_ref, sw_ref, og_ref,
                      zm_ref, y_ref, u_scr, m_scr, y_scr, *, tm, rows, width, tiles_per_seq):
    groups = width // LANES

    @pl.when(pl.program_id(0) % tiles_per_seq == 0)
    def _():
        u_scr[0:CONF_PAD_ROWS, :] = jnp.zeros((CONF_PAD_ROWS, width), F32)
        m_scr[0:SC_PAD_ROWS, :] = jnp.zeros((SC_PAD_ROWS, width), F32)

    n = _rms(x_ref[...], g_ref[...]).astype(BF16)
    zc = jnp.dot(n, wc_ref[...], preferred_element_type=F32)
    u_scr[CONF_PAD_ROWS:CONF_PAD_ROWS + tm, :] = zc[:, 0:width] * jax.nn.sigmoid(zc[:, width:2 * width])
    sb = zc[:, 2 * width:3 * width]
    m_scr[SC_PAD_ROWS:SC_PAD_ROWS + tm, :] = zc[:, 3 * width:4 * width] * zc[:, 4 * width:5 * width]

    def causal_taps(src_ref, w_ref, t0, lanes, pad_rows, taps, acc):
        win_rows = rows + pad_rows
        win = src_ref[t0:t0 + win_rows, lanes]
        for r in range(SUBLANES):
            shifted = None
            for j in range(taps):
                off = pad_rows - (taps - 1) + j
                if off % SUBLANES != r:
                    continue
                if shifted is None:
                    shifted = win if r == 0 else pltpu.roll(win, win_rows - r, 0)
                base = off - r
                acc = acc + w_ref[j:j + 1, lanes] * shifted[base:base + rows]
        return acc

    for c in range(tm // rows):
        t0 = c * rows
        for cg in range(groups):
            lanes = slice(cg * LANES, (cg + 1) * LANES)
            acc = jnp.broadcast_to(cb_ref[:, lanes], (rows, LANES))
            y_scr[:, lanes] = causal_taps(u_scr, cw_ref, t0, lanes, CONF_PAD_ROWS, CONF_KERNEL, acc)
        y = y_scr[...]
        mu = jnp.mean(y, axis=-1, keepdims=True)
        dlt = y - mu
        var = jnp.mean(dlt * dlt, axis=-1, keepdims=True)
        ln = dlt * lax.rsqrt(var + LN_EPS) * lg_ref[...] + lb_ref[...]
        conf = ln * jax.nn.sigmoid(ln)
        for cg in range(groups):
            lanes = slice(cg * LANES, (cg + 1) * LANES)
            y_ref[t0:t0 + rows, lanes] = _rms(conf[:, lanes], og_ref[cg:cg + 1, :]).astype(BF16)
        for cg in range(groups):
            lanes = slice(cg * LANES, (cg + 1) * LANES)
            acc = causal_taps(m_scr, sw_ref, t0, lanes, SC_PAD_ROWS, SC_KERNEL,
                              jnp.zeros((rows, LANES), F32))
            sc = sb[t0:t0 + rows, lanes] * acc
            y_ref[t0:t0 + rows, width + cg * LANES:width + (cg + 1) * LANES] = _rms(
                sc, og_ref[groups + cg:groups + cg + 1, :]).astype(BF16)

    u_scr[0:CONF_PAD_ROWS, :] = u_scr[tm:tm + CONF_PAD_ROWS, :]
    m_scr[0:SC_PAD_ROWS, :] = m_scr[tm:tm + SC_PAD_ROWS, :]
    zm_ref[...] = jnp.dot(n, wm_ref[...], preferred_element_type=F32).astype(BF16)


def _inproj_conv(h, gains, w_conv, w_mla, conf_w, conf_b, ln_g, ln_b, sc_w, out_gains, *, layer, seq, width,
                 tm=512, rows=128):
    m, d = h.shape
    depth = conf_w.shape[0]
    nc, nm = w_conv.shape[2], w_mla.shape[2]
    tm = min(tm, seq)
    rows = min(rows, tm)
    groups = width // LANES
    assert m % tm == 0 and seq % tm == 0 and tm % rows == 0 and nc == 5 * width
    assert conf_w.shape[1:] == (CONF_KERNEL, width) and sc_w.shape[1:] == (SC_KERNEL, width)
    assert out_gains.shape[1] % (2 * groups) == 0
    gain_block = out_gains.shape[1] // (2 * groups) - 1

    def full(shape):
        return pl.BlockSpec((None,) + shape, lambda i: (layer,) + (0,) * len(shape))

    est = (2 * tm * d * 4 + 2 * d * (nc + nm) * 2 + 2 * tm * (nm + 2 * width) * 2 + tm * (nc + nm) * 4
           + tm * d * 2 + (2 * tm + CONF_PAD_ROWS + SC_PAD_ROWS + rows) * width * 4)
    return pl.pallas_call(
        functools.partial(_inproj_conv_body, tm=tm, rows=rows, width=width, tiles_per_seq=seq // tm),
        grid=(m // tm,),
        in_specs=[
            pl.BlockSpec((tm, d), lambda i: (i, 0)),
            full((1, d)), full((d, nc)), full((d, nm)),
            full((CONF_KERNEL, width)), full((1, width)), full((1, width)), full((1, width)),
            full((SC_KERNEL, width)),
            pl.BlockSpec((None, 2 * groups, GROUP_DIM), lambda i: (layer, gain_block, 0)),
        ],
        out_specs=[pl.BlockSpec((tm, nm), lambda i: (i, 0)), pl.BlockSpec((tm, 2 * width), lambda i: (i, 0))],
        out_shape=[jax.ShapeDtypeStruct((m, nm), BF16), jax.ShapeDtypeStruct((m, 2 * width), BF16)],
        scratch_shapes=[pltpu.VMEM((CONF_PAD_ROWS + tm, width), F32),
                        pltpu.VMEM((SC_PAD_ROWS + tm, width), F32),
                        pltpu.VMEM((rows, width), F32)],
        compiler_params=pltpu.CompilerParams(
            dimension_semantics=("arbitrary",), vmem_limit_bytes=_vmem_limit(est)),
        name="inproj_conv",
    )(h, gains.reshape(depth, 1, d), w_conv, w_mla, conf_w, conf_b.reshape(depth, 1, width),
      ln_g.reshape(depth, 1, width), ln_b.reshape(depth, 1, width), sc_w, out_gains)


def _outproj_body(ya_ref, yc_ref, wa_ref, wc_ref, h_ref, o_ref):
    o_ref[...] = (h_ref[...]
                  + jnp.dot(ya_ref[...], wa_ref[...], preferred_element_type=F32)
                  + jnp.dot(yc_ref[...], wc_ref[...], preferred_element_type=F32))


def _outproj(y_attn, y_conv, w_o, h, *, layer, tm=1024):
    m, d = h.shape
    ka, kc = y_attn.shape[1], y_conv.shape[1]
    tm = min(tm, m)
    assert m % tm == 0 and ka == kc and ka + kc == w_o.shape[1]
    est = 2 * tm * (ka + kc) * 2 + 2 * (ka + kc) * d * 2 + 4 * tm * d * 4 + tm * d * 4
    return pl.pallas_call(
        _outproj_body,
        grid=(m // tm,),
        in_specs=[
            pl.BlockSpec((tm, ka), lambda i: (i, 0)),
            pl.BlockSpec((tm, kc), lambda i: (i, 0)),
            pl.BlockSpec((None, ka, d), lambda i: (layer, 0, 0)),
            pl.BlockSpec((None, kc, d), lambda i: (layer, 1, 0)),
            pl.BlockSpec((tm, d), lambda i: (i, 0)),
        ],
        out_specs=pl.BlockSpec((tm, d), lambda i: (i, 0)),
        out_shape=jax.ShapeDtypeStruct((m, d), F32),
        compiler_params=pltpu.CompilerParams(
            dimension_semantics=("arbitrary",), vmem_limit_bytes=_vmem_limit(est)),
        name="outproj",
    )(y_attn, y_conv, w_o, w_o, h)


def _rope_tables(seq):
    half = QK_ROPE_DIM // 2
    inv = 1.0 / (ROPE_THETA ** (jnp.arange(0, QK_ROPE_DIM, 2, dtype=F32) / QK_ROPE_DIM))
    ang = jnp.arange(seq, dtype=F32)[:, None] * inv[None, :]
    cos, sin = jnp.cos(ang), jnp.sin(ang)
    zero = jnp.zeros((seq, half), F32)
    c_tab = jnp.concatenate([cos, cos, zero, zero], axis=-1)
    s_lo = jnp.concatenate([-sin, zero, zero, zero], axis=-1)
    s_hi = jnp.concatenate([zero, sin, zero, zero], axis=-1)
    return c_tab, s_lo, s_hi


def kernel(x, ffn1_norm, ffn1_w_gate, ffn1_w_up, ffn1_w_down, mix_norm, w_in, q_norm, w_uq, kv_norm, w_ukv, conf_dw_w, conf_dw_b, conf_ln_g, conf_ln_b, sc_w, mix_out_norm, w_o, ffn2_norm, ffn2_w_gate, ffn2_w_up, ffn2_w_down, final_norm):
    batch, seq, d = x.shape
    depth = w_in.shape[0]
    q_rank = q_norm.shape[1]
    kv_rank = kv_norm.shape[1]
    conf_width = conf_dw_w.shape[2]
    sc_width = sc_w.shape[2]
    heads = w_ukv.shape[2] // (QK_NOPE_DIM + V_HEAD_DIM)
    assert w_uq.shape[2] == heads * (QK_NOPE_DIM + QK_ROPE_DIM)
    assert conf_width == sc_width == q_rank == kv_rank
    assert w_in.shape[2] == q_rank + kv_rank + QK_ROPE_DIM + 2 * conf_width + 3 * sc_width
    assert heads * V_HEAD_DIM + conf_width + sc_width == w_o.shape[1]

    c_tab, s_lo, s_hi = _rope_tables(seq)
    h = x.reshape(batch * seq, d)
    rope_at = q_rank + kv_rank
    rope_end = rope_at + QK_ROPE_DIM
    w_in_b = w_in.astype(BF16)
    w_mla = jnp.pad(w_in_b[:, :, :rope_end], ((0, 0), (0, 0), (0, LANES - QK_ROPE_DIM)))
    w_conv = w_in_b[:, :, rope_end:]
    wq = w_uq.reshape(depth, q_rank, heads, QK_NOPE_DIM + QK_ROPE_DIM)
    wq_p = jnp.pad(wq, ((0, 0), (0, 0), (0, 0), (0, QK_PAD_DIM - QK_NOPE_DIM - QK_ROPE_DIM))).reshape(
        depth, q_rank, heads * QK_PAD_DIM).astype(BF16)
    wkv = w_ukv.astype(BF16)
    w_o_b = w_o.astype(BF16)

    for l in range(depth):
        h = _ffn(h, ffn1_norm, ffn1_w_gate, ffn1_w_up, ffn1_w_down, final_norm, layer=l, final_norm=False,
                 in_place=(l > 0))
        z_mla, y_conv = _inproj_conv(h, mix_norm, w_conv, w_mla, conf_dw_w, conf_dw_b, conf_ln_g, conf_ln_b,
                                     sc_w, mix_out_norm, layer=l, seq=seq, width=conf_width)
        q, k, v = _mla_up(z_mla, q_norm, kv_norm, wq_p, wkv, c_tab, s_lo, s_hi, layer=l,
                          seq=seq, heads=heads, q_rank=q_rank, kv_rank=kv_rank)
        y_attn = _attention(q, k, v, mix_out_norm, layer=l, batch=batch, seq=seq, heads=heads)
        h = _outproj(y_attn.reshape(batch * seq, -1), y_conv, w_o_b, h, layer=l)
        h = _ffn(h, ffn2_norm, ffn2_w_gate, ffn2_w_up, ffn2_w_down, final_norm, layer=l,
                 final_norm=(l == depth - 1), in_place=True)
    return h.reshape(batch, seq, d)
```

```python
import functools

import jax
import jax.numpy as jnp
from jax import lax
from jax.experimental import pallas as pl
from jax.experimental.pallas import tpu as pltpu

F32 = jnp.float32
BF16 = jnp.bfloat16

LANES = 128
SUBLANES = 8
VMEM_PHYSICAL_BYTES = 64 * 1024 * 1024

GROUP_DIM = 128
QK_NOPE_DIM = 128
QK_ROPE_DIM = 64
V_HEAD_DIM = 128
QK_PAD_DIM = 256
ROPE_THETA = 10000.0
CONF_KERNEL = 31
SC_KERNEL = 3
RMS_EPS = 1e-6
LN_EPS = 1e-5
LOG2_E = 1.4426950408889634

CONF_PAD_ROWS = 32
SC_PAD_ROWS = 8


def _vmem_limit(estimate_bytes):
    return int(min(estimate_bytes * 5 // 4 + (4 << 20), VMEM_PHYSICAL_BYTES - (2 << 20)))


def _rms(x, gain):
    return x * lax.rsqrt(jnp.mean(x * x, axis=-1, keepdims=True) + RMS_EPS) * gain


def _ffn_step(x_ref, g_ref, load_weights, fg_ref, o_ref, n_ref, final_norm):
    f = pl.program_id(1)

    def half_swiglu(n):
        wg, wu, wd = load_weights()
        gate = jnp.dot(n, wg, preferred_element_type=F32)
        up = jnp.dot(n, wu, preferred_element_type=F32)
        act = (gate * jax.nn.sigmoid(gate) * up * 0.5).astype(BF16)
        return jnp.dot(act, wd, preferred_element_type=F32)

    @pl.when(f == 0)
    def _():
        x = x_ref[...]
        n = _rms(x, g_ref[...]).astype(BF16)
        n_ref[...] = n
        o_ref[...] = x + half_swiglu(n)

    @pl.when(f != 0)
    def _():
        o_ref[...] += half_swiglu(n_ref[...])

    if final_norm:
        @pl.when(f == pl.num_programs(1) - 1)
        def _():
            o_ref[...] = _rms(o_ref[...], fg_ref[...])


def _ffn_first_body(x_ref, g_ref, wg_ref, wu_ref, wd_ref, fg_ref, o_ref, wgo_ref, wuo_ref, wdo_ref, n_ref,
                    *, final_norm):
    def load_weights():
        wg = wg_ref[...].astype(BF16)
        wu = wu_ref[...].astype(BF16)
        wd = wd_ref[...].astype(BF16)
        wgo_ref[...] = wg
        wuo_ref[...] = wu
        wdo_ref[...] = wd
        return wg, wu, wd

    _ffn_step(x_ref, g_ref, load_weights, fg_ref, o_ref, n_ref, final_norm)


def _ffn_rest_body(x_ref, g_ref, wg_ref, wu_ref, wd_ref, fg_ref, o_ref, n_ref, *, final_norm):
    _ffn_step(x_ref, g_ref, lambda: (wg_ref[...], wu_ref[...], wd_ref[...]), fg_ref, o_ref, n_ref,
              final_norm)


def _ffn_rest_copy_body(x_ref, g_ref, wg_ref, wu_ref, wd_ref, fg_ref, first_ref, o_ref, n_ref):
    i = pl.program_id(0)

    @pl.when((i == 0) & (pl.program_id(1) == 0))
    def _():
        pltpu.sync_copy(first_ref, o_ref)

    @pl.when(i > 0)
    def _():
        _ffn_step(x_ref, g_ref, lambda: (wg_ref[...], wu_ref[...], wd_ref[...]), fg_ref, o_ref, n_ref,
                  False)


def _ffn(h, gains, w_gate, w_up, w_down, final_gain, *, layer, final_norm, in_place,
         tm=1024, tf=512, tf_first=256):
    m, d = h.shape
    ff = w_gate.shape[2]
    tm = min(tm, m)
    assert m % tm == 0 and ff % tf == 0 and ff % tf_first == 0
    assert in_place or not final_norm
    name = "ffn_final" if final_norm else "ffn"
    vec = pl.BlockSpec((None, 1, d), lambda i, f: (layer, 0, 0))
    fvec = pl.BlockSpec((1, d), lambda i, f: (0, 0))
    gains3 = gains.reshape(gains.shape[0], 1, d)
    fg = final_gain.reshape(1, d)
    scratch = [pltpu.VMEM((tm, d), BF16)]
    sem = ("arbitrary", "arbitrary")

    est = (4 * tm * d * 4 + tm * d * 2 + 2 * 3 * d * tf_first * (4 + 2) + 3 * d * tf_first * 2
           + tm * tf_first * (4 + 4 + 2))
    first, wg, wu, wd = pl.pallas_call(
        functools.partial(_ffn_first_body, final_norm=final_norm),
        grid=(1, ff // tf_first),
        in_specs=[
            pl.BlockSpec((tm, d), lambda i, f: (0, 0)),
            vec,
            pl.BlockSpec((None, d, tf_first), lambda i, f: (layer, 0, f)),
            pl.BlockSpec((None, d, tf_first), lambda i, f: (layer, 0, f)),
            pl.BlockSpec((None, tf_first, d), lambda i, f: (layer, f, 0)),
            fvec,
        ],
        out_specs=[
            pl.BlockSpec((tm, d), lambda i, f: (0, 0)),
            pl.BlockSpec((d, tf_first), lambda i, f: (0, f)),
            pl.BlockSpec((d, tf_first), lambda i, f: (0, f)),
            pl.BlockSpec((tf_first, d), lambda i, f: (f, 0)),
        ],
        out_shape=[
            jax.ShapeDtypeStruct((m, d) if in_place else (tm, d), F32),
            jax.ShapeDtypeStruct((d, ff), BF16),
            jax.ShapeDtypeStruct((d, ff), BF16),
            jax.ShapeDtypeStruct((ff, d), BF16),
        ],
        input_output_aliases={0: 0} if in_place else {},
        scratch_shapes=scratch,
        compiler_params=pltpu.CompilerParams(dimension_semantics=sem, vmem_limit_bytes=_vmem_limit(est)),
        name=name + "_first",
    )(h, gains3, w_gate, w_up, w_down, fg)
    if m == tm:
        return first

    est = 4 * tm * d * 4 + tm * d * 2 + 2 * 3 * d * tf * 2 + tm * tf * (4 + 4 + 2)
    params = pltpu.CompilerParams(dimension_semantics=sem, vmem_limit_bytes=_vmem_limit(est))
    if in_place:
        return pl.pallas_call(
            functools.partial(_ffn_rest_body, final_norm=final_norm),
            grid=(m // tm - 1, ff // tf),
            in_specs=[
                pl.BlockSpec((tm, d), lambda i, f: (i + 1, 0)),
                vec,
                pl.BlockSpec((d, tf), lambda i, f: (0, f)),
                pl.BlockSpec((d, tf), lambda i, f: (0, f)),
                pl.BlockSpec((tf, d), lambda i, f: (f, 0)),
                fvec,
            ],
            out_specs=pl.BlockSpec((tm, d), lambda i, f: (i + 1, 0)),
            out_shape=jax.ShapeDtypeStruct((m, d), F32),
            input_output_aliases={0: 0},
            scratch_shapes=scratch,
            compiler_params=params,
            name=name,
        )(first, gains3, wg, wu, wd, fg)

    def hidden(i, f):
        return jnp.where(i == 0, 0, f)

    return pl.pallas_call(
        _ffn_rest_copy_body,
        grid=(m // tm, ff // tf),
        in_specs=[
            pl.BlockSpec((tm, d), lambda i, f: (jnp.maximum(i, 1), 0)),
            vec,
            pl.BlockSpec((d, tf), lambda i, f: (0, hidden(i, f))),
            pl.BlockSpec((d, tf), lambda i, f: (0, hidden(i, f))),
            pl.BlockSpec((tf, d), lambda i, f: (hidden(i, f), 0)),
            fvec,
            pl.BlockSpec(memory_space=pl.ANY),
        ],
        out_specs=pl.BlockSpec((tm, d), lambda i, f: (i, 0)),
        out_shape=jax.ShapeDtypeStruct((m, d), F32),
        scratch_shapes=scratch,
        compiler_params=params,
        name=name + "_assemble",
    )(h, gains3, wg, wu, wd, fg, first)


def _rope128(x, c_tab, s_lo, s_hi):
    half = QK_ROPE_DIM // 2
    partner_hi = pltpu.roll(x, LANES - half, 1)
    partner_lo = pltpu.roll(x, half, 1)
    return x * c_tab + partner_hi * s_lo + partner_lo * s_hi


def _mla_up_body(cq_ref, ckv_ref, kr_ref, qg_ref, kvg_ref, wq_ref, wkv_ref,
                 c_ref, slo_ref, shi_ref, q_ref, k_ref, v_ref, *, heads):
    c_tab, s_lo, s_hi = c_ref[...], slo_ref[...], shi_ref[...]

    scale = float(QK_NOPE_DIM + QK_ROPE_DIM) ** -0.5 * LOG2_E
    nq = _rms(cq_ref[...].astype(F32), qg_ref[...] * scale).astype(BF16)
    q = jnp.dot(nq, wq_ref[...], preferred_element_type=F32)
    nkv = _rms(ckv_ref[...].astype(F32), kvg_ref[...]).astype(BF16)
    kv = jnp.dot(nkv, wkv_ref[...], preferred_element_type=F32)
    k_rope = _rope128(kr_ref[...].astype(F32), c_tab, s_lo, s_hi).astype(BF16)

    for hd in range(heads):
        base = hd * QK_PAD_DIM
        q_ref[:, base:base + LANES] = q[:, base:base + LANES].astype(BF16)
        q_ref[:, base + LANES:base + 2 * LANES] = _rope128(
            q[:, base + LANES:base + 2 * LANES], c_tab, s_lo, s_hi).astype(BF16)
        k_ref[:, base:base + LANES] = kv[:, base:base + LANES].astype(BF16)
        k_ref[:, base + LANES:base + 2 * LANES] = k_rope
        v_ref[:, hd * LANES:(hd + 1) * LANES] = kv[:, base + LANES:base + 2 * LANES].astype(BF16)


def _mla_up(z, q_gains, kv_gains, wq, wkv, c_tab, s_lo, s_hi, *, layer, seq, heads, q_rank, kv_rank,
            tm=1024):
    m, nw = z.shape
    depth = q_gains.shape[0]
    tm = min(tm, seq)
    assert m % tm == 0 and seq % tm == 0 and q_rank == kv_rank
    s_blocks = seq // tm
    kr_block = (nw - LANES) // LANES
    hq = heads * QK_PAD_DIM
    hv = heads * V_HEAD_DIM
    est = (2 * (2 * tm * q_rank * 2 + tm * LANES * 2) + 2 * 2 * q_rank * hq * 2
           + 2 * 3 * tm * LANES * 4 + 2 * tm * (2 * hq + hv) * 2 + 2 * tm * hq * 4)
    tab_spec = pl.BlockSpec((tm, LANES), lambda i: (i % s_blocks, 0))
    return pl.pallas_call(
        functools.partial(_mla_up_body, heads=heads),
        grid=(m // tm,),
        in_specs=[
            pl.BlockSpec((tm, q_rank), lambda i: (i, 0)),
            pl.BlockSpec((tm, kv_rank), lambda i: (i, 1)),
            pl.BlockSpec((tm, LANES), lambda i: (i, kr_block)),
            pl.BlockSpec((None, 1, q_rank), lambda i: (layer, 0, 0)),
            pl.BlockSpec((None, 1, kv_rank), lambda i: (layer, 0, 0)),
            pl.BlockSpec((None, q_rank, hq), lambda i: (layer, 0, 0)),
            pl.BlockSpec((None, kv_rank, hq), lambda i: (layer, 0, 0)),
            tab_spec, tab_spec, tab_spec,
        ],
        out_specs=[
            pl.BlockSpec((tm, hq), lambda i: (i, 0)),
            pl.BlockSpec((tm, hq), lambda i: (i, 0)),
            pl.BlockSpec((tm, hv), lambda i: (i, 0)),
        ],
        out_shape=[
            jax.ShapeDtypeStruct((m, hq), BF16),
            jax.ShapeDtypeStruct((m, hq), BF16),
            jax.ShapeDtypeStruct((m, hv), BF16),
        ],
        compiler_params=pltpu.CompilerParams(
            dimension_semantics=("arbitrary",), vmem_limit_bytes=_vmem_limit(est)),
        name="mla_up",
    )(z, z, z, q_gains.reshape(depth, 1, q_rank), kv_gains.reshape(depth, 1, kv_rank), wq, wkv,
      c_tab, s_lo, s_hi)


def _attn_body(q_ref, k_ref, v_ref, g_ref, o_ref, vx_scr, bias_scr, *, seq, tq, hps):
    @pl.when((pl.program_id(0) == 0) & (pl.program_id(1) == 0))
    def _():
        row = lax.broadcasted_iota(jnp.int32, (tq, tq), 0)
        col = lax.broadcasted_iota(jnp.int32, (tq, tq), 1)
        bias_scr[...] = jnp.where(col <= row, 0.0, -jnp.inf).astype(F32)
        vx_scr[:, :, V_HEAD_DIM:] = jnp.ones((hps, seq, V_HEAD_DIM), BF16)

    nt = (((1,), (1,)), ((), ()))
    for hh in range(hps):
        vx_scr[hh, :, :V_HEAD_DIM] = v_ref[:, hh * V_HEAD_DIM:(hh + 1) * V_HEAD_DIM]
    for hh in range(hps):
        gain = g_ref[hh]
        qk_lanes = slice(hh * QK_PAD_DIM, (hh + 1) * QK_PAD_DIM)
        for qi in range(seq // tq):
            q0, q1 = qi * tq, (qi + 1) * tq
            q = q_ref[q0:q1, qk_lanes]
            s = lax.dot_general(q, k_ref[q0:q1, qk_lanes], nt, preferred_element_type=F32) + bias_scr[...]
            m = jnp.max(s, axis=-1, keepdims=True)
            acc = jnp.dot(jnp.exp2(s - m).astype(BF16), vx_scr[hh, q0:q1, :], preferred_element_type=F32)
            for kj in range(qi):
                k0, k1 = kj * tq, (kj + 1) * tq
                s = lax.dot_general(q, k_ref[k0:k1, qk_lanes], nt, preferred_element_type=F32)
                m_new = jnp.maximum(m, jnp.max(s, axis=-1, keepdims=True))
                acc = acc * jnp.exp2(m - m_new) + jnp.dot(
                    jnp.exp2(s - m_new).astype(BF16), vx_scr[hh, k0:k1, :], preferred_element_type=F32)
                m = m_new
            o = acc[:, :V_HEAD_DIM] / acc[:, V_HEAD_DIM:]
            o_ref[q0:q1, hh * V_HEAD_DIM:(hh + 1) * V_HEAD_DIM] = _rms(o, gain).astype(BF16)


def _attention(q, k, v, gains, *, layer, batch, seq, heads, tq=512, hps=4):
    tq = min(tq, seq)
    n_groups = gains.shape[1]
    assert seq % tq == 0 and heads % hps == 0 and n_groups % hps == 0
    gain_block0 = layer * (n_groups // hps)
    q3 = q.reshape(batch, seq, heads * QK_PAD_DIM)
    k3 = k.reshape(batch, seq, heads * QK_PAD_DIM)
    v3 = v.reshape(batch, seq, heads * V_HEAD_DIM)
    est = hps * (2 * (2 * seq * QK_PAD_DIM * 2 + 2 * seq * V_HEAD_DIM * 2) + seq * 2 * V_HEAD_DIM * 2
                 + 6 * tq * tq * 4) + tq * tq * 4
    return pl.pallas_call(
        functools.partial(_attn_body, seq=seq, tq=tq, hps=hps),
        grid=(batch, heads // hps),
        in_specs=[
            pl.BlockSpec((None, seq, hps * QK_PAD_DIM), lambda b, h: (b, 0, h)),
            pl.BlockSpec((None, seq, hps * QK_PAD_DIM), lambda b, h: (b, 0, h)),
            pl.BlockSpec((None, seq, hps * V_HEAD_DIM), lambda b, h: (b, 0, h)),
            pl.BlockSpec((hps, 1, GROUP_DIM), lambda b, h: (gain_block0 + h, 0, 0)),
        ],
        out_specs=pl.BlockSpec((None, seq, hps * V_HEAD_DIM), lambda b, h: (b, 0, h)),
        out_shape=jax.ShapeDtypeStruct((batch, seq, heads * V_HEAD_DIM), BF16),
        scratch_shapes=[pltpu.VMEM((hps, seq, 2 * V_HEAD_DIM), BF16), pltpu.VMEM((tq, tq), F32)],
        compiler_params=pltpu.CompilerParams(
            dimension_semantics=("arbitrary", "arbitrary"), vmem_limit_bytes=_vmem_limit(est)),
        name="attention",
    )(q3, k3, v3, gains.reshape(-1, 1, GROUP_DIM))


def _inproj_conv_body(x_ref, g_ref, w_ref, cw_ref, cb_ref, lg_ref, lb_ref, sw_ref, og_ref,
                      zm_ref, y_ref, u_scr, m_scr, y_scr, wc_scr, *, tm, rows, width, tiles_per_seq, nm):
    groups = width // LANES
    conv0 = nm - LANES + QK_ROPE_DIM

    @pl.when(pl.program_id(0) == 0)
    def _():
        wc_scr[...] = w_ref[:, conv0:conv0 + 5 * width]

    @pl.when(pl.program_id(0) % tiles_per_seq == 0)
    def _():
        u_scr[0:CONF_PAD_ROWS, :] = jnp.zeros((CONF_PAD_ROWS, width), F32)
        m_scr[0:SC_PAD_ROWS, :] = jnp.zeros((SC_PAD_ROWS, width), F32)

    n = _rms(x_ref[...], g_ref[...]).astype(BF16)
    zc = jnp.dot(n, wc_scr[...], preferred_element_type=F32)
    u_scr[CONF_PAD_ROWS:CONF_PAD_ROWS + tm, :] = zc[:, 0:width] * jax.nn.sigmoid(zc[:, width:2 * width])
    sb = zc[:, 2 * width:3 * width]
    m_scr[SC_PAD_ROWS:SC_PAD_ROWS + tm, :] = zc[:, 3 * width:4 * width] * zc[:, 4 * width:5 * width]

    def causal_taps(src_ref, w_ref, t0, lanes, pad_rows, taps, acc):
        win_rows = rows + pad_rows
        win = src_ref[t0:t0 + win_rows, lanes]
        for r in range(SUBLANES):
            shifted = None
            for j in range(taps):
                off = pad_rows - (taps - 1) + j
                if off % SUBLANES != r:
                    continue
                if shifted is None:
                    shifted = win if r == 0 else pltpu.roll(win, win_rows - r, 0)
                base = off - r
                acc = acc + w_ref[j:j + 1, lanes] * shifted[base:base + rows]
        return acc

    for c in range(tm // rows):
        t0 = c * rows
        for cg in range(groups):
            lanes = slice(cg * LANES, (cg + 1) * LANES)
            acc = jnp.broadcast_to(cb_ref[:, lanes], (rows, LANES))
            y_scr[:, lanes] = causal_taps(u_scr, cw_ref, t0, lanes, CONF_PAD_ROWS, CONF_KERNEL, acc)
        y = y_scr[...]
        mu = jnp.mean(y, axis=-1, keepdims=True)
        dlt = y - mu
        var = jnp.mean(dlt * dlt, axis=-1, keepdims=True)
        ln = dlt * lax.rsqrt(var + LN_EPS) * lg_ref[...] + lb_ref[...]
        conf = ln * jax.nn.sigmoid(ln)
        for cg in range(groups):
            lanes = slice(cg * LANES, (cg + 1) * LANES)
            y_ref[t0:t0 + rows, lanes] = _rms(conf[:, lanes], og_ref[cg:cg + 1, :]).astype(BF16)
        for cg in range(groups):
            lanes = slice(cg * LANES, (cg + 1) * LANES)
            acc = causal_taps(m_scr, sw_ref, t0, lanes, SC_PAD_ROWS, SC_KERNEL,
                              jnp.zeros((rows, LANES), F32))
            sc = sb[t0:t0 + rows, lanes] * acc
            y_ref[t0:t0 + rows, width + cg * LANES:width + (cg + 1) * LANES] = _rms(
                sc, og_ref[groups + cg:groups + cg + 1, :]).astype(BF16)

    u_scr[0:CONF_PAD_ROWS, :] = u_scr[tm:tm + CONF_PAD_ROWS, :]
    m_scr[0:SC_PAD_ROWS, :] = m_scr[tm:tm + SC_PAD_ROWS, :]
    zm_ref[...] = jnp.dot(n, w_ref[:, 0:nm], preferred_element_type=F32).astype(BF16)


def _inproj_conv(h, gains, w_in_b, conf_w, conf_b, ln_g, ln_b, sc_w, out_gains, *, layer, seq, width,
                 tm=512, rows=128):
    m, d = h.shape
    depth = conf_w.shape[0]
    nw = w_in_b.shape[2]
    nc = 5 * width
    nm = nw - nc - QK_ROPE_DIM + LANES
    tm = min(tm, seq)
    rows = min(rows, tm)
    groups = width // LANES
    assert m % tm == 0 and seq % tm == 0 and tm % rows == 0 and nm % LANES == 0
    assert conf_w.shape[1:] == (CONF_KERNEL, width) and sc_w.shape[1:] == (SC_KERNEL, width)
    assert out_gains.shape[1] % (2 * groups) == 0
    gain_block = out_gains.shape[1] // (2 * groups) - 1

    def full(shape):
        return pl.BlockSpec((None,) + shape, lambda i: (layer,) + (0,) * len(shape))

    est = (2 * tm * d * 4 + 2 * d * nw * 2 + d * nc * 2 + 2 * tm * (nm + 2 * width) * 2 + tm * (nc + nm) * 4
           + tm * d * 2 + (2 * tm + CONF_PAD_ROWS + SC_PAD_ROWS + rows) * width * 4)
    return pl.pallas_call(
        functools.partial(_inproj_conv_body, tm=tm, rows=rows, width=width, tiles_per_seq=seq // tm, nm=nm),
        grid=(m // tm,),
        in_specs=[
            pl.BlockSpec((tm, d), lambda i: (i, 0)),
            full((1, d)), full((d, nw)),
            full((CONF_KERNEL, width)), full((1, width)), full((1, width)), full((1, width)),
            full((SC_KERNEL, width)),
            pl.BlockSpec((None, 2 * groups, GROUP_DIM), lambda i: (layer, gain_block, 0)),
        ],
        out_specs=[pl.BlockSpec((tm, nm), lambda i: (i, 0)), pl.BlockSpec((tm, 2 * width), lambda i: (i, 0))],
        out_shape=[jax.ShapeDtypeStruct((m, nm), BF16), jax.ShapeDtypeStruct((m, 2 * width), BF16)],
        scratch_shapes=[pltpu.VMEM((CONF_PAD_ROWS + tm, width), F32),
                        pltpu.VMEM((SC_PAD_ROWS + tm, width), F32),
                        pltpu.VMEM((rows, width), F32),
                        pltpu.VMEM((d, nc), BF16)],
        compiler_params=pltpu.CompilerParams(
            dimension_semantics=("arbitrary",), vmem_limit_bytes=_vmem_limit(est)),
        name="inproj_conv",
    )(h, gains.reshape(depth, 1, d), w_in_b, conf_w, conf_b.reshape(depth, 1, width),
      ln_g.reshape(depth, 1, width), ln_b.reshape(depth, 1, width), sc_w, out_gains)


def _outproj_body(ya_ref, yc_ref, wa_ref, wc_ref, h_ref, o_ref):
    o_ref[...] = (h_ref[...]
                  + jnp.dot(ya_ref[...], wa_ref[...], preferred_element_type=F32)
                  + jnp.dot(yc_ref[...], wc_ref[...], preferred_element_type=F32))


def _outproj(y_attn, y_conv, w_o, h, *, layer, tm=1024):
    m, d = h.shape
    ka, kc = y_attn.shape[1], y_conv.shape[1]
    tm = min(tm, m)
    assert m % tm == 0 and ka == kc and ka + kc == w_o.shape[1]
    est = 2 * tm * (ka + kc) * 2 + 2 * (ka + kc) * d * 2 + 4 * tm * d * 4 + tm * d * 4
    return pl.pallas_call(
        _outproj_body,
        grid=(m // tm,),
        in_specs=[
            pl.BlockSpec((tm, ka), lambda i: (i, 0)),
            pl.BlockSpec((tm, kc), lambda i: (i, 0)),
            pl.BlockSpec((None, ka, d), lambda i: (layer, 0, 0)),
            pl.BlockSpec((None, kc, d), lambda i: (layer, 1, 0)),
            pl.BlockSpec((tm, d), lambda i: (i, 0)),
        ],
        out_specs=pl.BlockSpec((tm, d), lambda i: (i, 0)),
        out_shape=jax.ShapeDtypeStruct((m, d), F32),
        compiler_params=pltpu.CompilerParams(
            dimension_semantics=("arbitrary",), vmem_limit_bytes=_vmem_limit(est)),
        name="outproj",
    )(y_attn, y_conv, w_o, w_o, h)


def _rope_tables(seq):
    half = QK_ROPE_DIM // 2
    inv = 1.0 / (ROPE_THETA ** (jnp.arange(0, QK_ROPE_DIM, 2, dtype=F32) / QK_ROPE_DIM))
    ang = jnp.arange(seq, dtype=F32)[:, None] * inv[None, :]
    cos, sin = jnp.cos(ang), jnp.sin(ang)
    zero = jnp.zeros((seq, half), F32)
    c_tab = jnp.concatenate([cos, cos, zero, zero], axis=-1)
    s_lo = jnp.concatenate([-sin, zero, zero, zero], axis=-1)
    s_hi = jnp.concatenate([zero, sin, zero, zero], axis=-1)
    return c_tab, s_lo, s_hi


def kernel(x, ffn1_norm, ffn1_w_gate, ffn1_w_up, ffn1_w_down, mix_norm, w_in, q_norm, w_uq, kv_norm, w_ukv, conf_dw_w, conf_dw_b, conf_ln_g, conf_ln_b, sc_w, mix_out_norm, w_o, ffn2_norm, ffn2_w_gate, ffn2_w_up, ffn2_w_down, final_norm):
    batch, seq, d = x.shape
    depth = w_in.shape[0]
    q_rank = q_norm.shape[1]
    kv_rank = kv_norm.shape[1]
    conf_width = conf_dw_w.shape[2]
    sc_width = sc_w.shape[2]
    heads = w_ukv.shape[2] // (QK_NOPE_DIM + V_HEAD_DIM)
    assert w_uq.shape[2] == heads * (QK_NOPE_DIM + QK_ROPE_DIM)
    assert conf_width == sc_width == q_rank == kv_rank
    assert w_in.shape[2] == q_rank + kv_rank + QK_ROPE_DIM + 2 * conf_width + 3 * sc_width
    assert heads * V_HEAD_DIM + conf_width + sc_width == w_o.shape[1]

    c_tab, s_lo, s_hi = _rope_tables(seq)
    h = x.reshape(batch * seq, d)
    w_in_b = w_in.astype(BF16)
    wq = w_uq.reshape(depth, q_rank, heads, QK_NOPE_DIM + QK_ROPE_DIM)
    wq_p = jnp.pad(wq, ((0, 0), (0, 0), (0, 0), (0, QK_PAD_DIM - QK_NOPE_DIM - QK_ROPE_DIM))).reshape(
        depth, q_rank, heads * QK_PAD_DIM).astype(BF16)
    wkv = w_ukv.astype(BF16)
    w_o_b = w_o.astype(BF16)

    for l in range(depth):
        h = _ffn(h, ffn1_norm, ffn1_w_gate, ffn1_w_up, ffn1_w_down, final_norm, layer=l, final_norm=False,
                 in_place=(l > 0))
        z_mla, y_conv = _inproj_conv(h, mix_norm, w_in_b, conf_dw_w, conf_dw_b, conf_ln_g, conf_ln_b,
                                     sc_w, mix_out_norm, layer=l, seq=seq, width=conf_width)
        q, k, v = _mla_up(z_mla, q_norm, kv_norm, wq_p, wkv, c_tab, s_lo, s_hi, layer=l,
                          seq=seq, heads=heads, q_rank=q_rank, kv_rank=kv_rank)
        y_attn = _attention(q, k, v, mix_out_norm, layer=l, batch=batch, seq=seq, heads=heads)
        h = _outproj(y_attn.reshape(batch * seq, -1), y_conv, w_o_b, h, layer=l)
        h = _ffn(h, ffn2_norm, ffn2_w_gate, ffn2_w_up, ffn2_w_down, final_norm, layer=l,
                 final_norm=(l == depth - 1), in_place=True)
    return h.reshape(batch, seq, d)
```

```python
import functools

import jax
import jax.numpy as jnp
from jax import lax
from jax.experimental import pallas as pl
from jax.experimental.pallas import tpu as pltpu

F32 = jnp.float32
BF16 = jnp.bfloat16

LANES = 128
SUBLANES = 8
VMEM_PHYSICAL_BYTES = 64 * 1024 * 1024

GROUP_DIM = 128
QK_NOPE_DIM = 128
QK_ROPE_DIM = 64
V_HEAD_DIM = 128
QK_PAD_DIM = 256
ROPE_THETA = 10000.0
CONF_KERNEL = 31
SC_KERNEL = 3
RMS_EPS = 1e-6
LN_EPS = 1e-5
LOG2_E = 1.4426950408889634

CONF_PAD_ROWS = 32
SC_PAD_ROWS = 8


def _vmem_limit(estimate_bytes):
    return int(min(estimate_bytes * 5 // 4 + (4 << 20), VMEM_PHYSICAL_BYTES - (2 << 20)))


def _rms(x, gain):
    return x * lax.rsqrt(jnp.mean(x * x, axis=-1, keepdims=True) + RMS_EPS) * gain


def _ffn_step(x_ref, g_ref, load_weights, fg_ref, o_ref, n_ref, final_norm):
    f = pl.program_id(1)

    def half_swiglu(n):
        wg, wu, wd = load_weights()
        gate = jnp.dot(n, wg, preferred_element_type=F32)
        up = jnp.dot(n, wu, preferred_element_type=F32)
        act = (gate * jax.nn.sigmoid(gate) * up * 0.5).astype(BF16)
        return jnp.dot(act, wd, preferred_element_type=F32)

    @pl.when(f == 0)
    def _():
        x = x_ref[...]
        n = _rms(x, g_ref[...]).astype(BF16)
        n_ref[...] = n
        o_ref[...] = x + half_swiglu(n)

    @pl.when(f != 0)
    def _():
        o_ref[...] += half_swiglu(n_ref[...])

    if final_norm:
        @pl.when(f == pl.num_programs(1) - 1)
        def _():
            o_ref[...] = _rms(o_ref[...], fg_ref[...])


def _ffn_first_body(x_ref, g_ref, wg_ref, wu_ref, wd_ref, fg_ref, o_ref, wgo_ref, wuo_ref, wdo_ref, n_ref,
                    *, final_norm):
    def load_weights():
        wg = wg_ref[...].astype(BF16)
        wu = wu_ref[...].astype(BF16)
        wd = wd_ref[...].astype(BF16)
        wgo_ref[...] = wg
        wuo_ref[...] = wu
        wdo_ref[...] = wd
        return wg, wu, wd

    _ffn_step(x_ref, g_ref, load_weights, fg_ref, o_ref, n_ref, final_norm)


def _ffn_rest_body(x_ref, g_ref, wg_ref, wu_ref, wd_ref, fg_ref, o_ref, n_ref, *, final_norm):
    _ffn_step(x_ref, g_ref, lambda: (wg_ref[...], wu_ref[...], wd_ref[...]), fg_ref, o_ref, n_ref,
              final_norm)


def _ffn_rest_copy_body(x_ref, g_ref, wg_ref, wu_ref, wd_ref, fg_ref, first_ref, o_ref, n_ref):
    i = pl.program_id(0)

    @pl.when((i == 0) & (pl.program_id(1) == 0))
    def _():
        pltpu.sync_copy(first_ref, o_ref)

    @pl.when(i > 0)
    def _():
        _ffn_step(x_ref, g_ref, lambda: (wg_ref[...], wu_ref[...], wd_ref[...]), fg_ref, o_ref, n_ref,
                  False)


def _ffn(h, gains, w_gate, w_up, w_down, final_gain, *, layer, final_norm, in_place,
         tm=1024, tf=512, tf_first=256):
    m, d = h.shape
    ff = w_gate.shape[2]
    tm = min(tm, m)
    assert m % tm == 0 and ff % tf == 0 and ff % tf_first == 0
    assert in_place or not final_norm
    name = "ffn_final" if final_norm else "ffn"
    vec = pl.BlockSpec((None, 1, d), lambda i, f: (layer, 0, 0))
    fvec = pl.BlockSpec((1, d), lambda i, f: (0, 0))
    gains3 = gains.reshape(gains.shape[0], 1, d)
    fg = final_gain.reshape(1, d)
    scratch = [pltpu.VMEM((tm, d), BF16)]
    sem = ("arbitrary", "arbitrary")

    est = (4 * tm * d * 4 + tm * d * 2 + 2 * 3 * d * tf_first * (4 + 2) + 3 * d * tf_first * 2
           + tm * tf_first * (4 + 4 + 2))
    first, wg, wu, wd = pl.pallas_call(
        functools.partial(_ffn_first_body, final_norm=final_norm),
        grid=(1, ff // tf_first),
        in_specs=[
            pl.BlockSpec((tm, d), lambda i, f: (0, 0)),
            vec,
            pl.BlockSpec((None, d, tf_first), lambda i, f: (layer, 0, f)),
            pl.BlockSpec((None, d, tf_first), lambda i, f: (layer, 0, f)),
            pl.BlockSpec((None, tf_first, d), lambda i, f: (layer, f, 0)),
            fvec,
        ],
        out_specs=[
            pl.BlockSpec((tm, d), lambda i, f: (0, 0)),
            pl.BlockSpec((d, tf_first), lambda i, f: (0, f)),
            pl.BlockSpec((d, tf_first), lambda i, f: (0, f)),
            pl.BlockSpec((tf_first, d), lambda i, f: (f, 0)),
        ],
        out_shape=[
            jax.ShapeDtypeStruct((m, d) if in_place else (tm, d), F32),
            jax.ShapeDtypeStruct((d, ff), BF16),
            jax.ShapeDtypeStruct((d, ff), BF16),
            jax.ShapeDtypeStruct((ff, d), BF16),
        ],
        input_output_aliases={0: 0} if in_place else {},
        scratch_shapes=scratch,
        compiler_params=pltpu.CompilerParams(dimension_semantics=sem, vmem_limit_bytes=_vmem_limit(est)),
        name=name + "_first",
    )(h, gains3, w_gate, w_up, w_down, fg)
    if m == tm:
        return first

    est = 4 * tm * d * 4 + tm * d * 2 + 2 * 3 * d * tf * 2 + tm * tf * (4 + 4 + 2)
    params = pltpu.CompilerParams(dimension_semantics=sem, vmem_limit_bytes=_vmem_limit(est))
    if in_place:
        return pl.pallas_call(
            functools.partial(_ffn_rest_body, final_norm=final_norm),
            grid=(m // tm - 1, ff // tf),
            in_specs=[
                pl.BlockSpec((tm, d), lambda i, f: (i + 1, 0)),
                vec,
                pl.BlockSpec((d, tf), lambda i, f: (0, f)),
                pl.BlockSpec((d, tf), lambda i, f: (0, f)),
                pl.BlockSpec((tf, d), lambda i, f: (f, 0)),
                fvec,
            ],
            out_specs=pl.BlockSpec((tm, d), lambda i, f: (i + 1, 0)),
            out_shape=jax.ShapeDtypeStruct((m, d), F32),
            input_output_aliases={0: 0},
            scratch_shapes=scratch,
            compiler_params=params,
            name=name,
        )(first, gains3, wg, wu, wd, fg)

    def hidden(i, f):
        return jnp.where(i == 0, 0, f)

    return pl.pallas_call(
        _ffn_rest_copy_body,
        grid=(m // tm, ff // tf),
        in_specs=[
            pl.BlockSpec((tm, d), lambda i, f: (jnp.maximum(i, 1), 0)),
            vec,
            pl.BlockSpec((d, tf), lambda i, f: (0, hidden(i, f))),
            pl.BlockSpec((d, tf), lambda i, f: (0, hidden(i, f))),
            pl.BlockSpec((tf, d), lambda i, f: (hidden(i, f), 0)),
            fvec,
            pl.BlockSpec(memory_space=pl.ANY),
        ],
        out_specs=pl.BlockSpec((tm, d), lambda i, f: (i, 0)),
        out_shape=jax.ShapeDtypeStruct((m, d), F32),
        scratch_shapes=scratch,
        compiler_params=params,
        name=name + "_assemble",
    )(h, gains3, wg, wu, wd, fg, first)


def _rope128(x, c_tab, s_lo, s_hi):
    half = QK_ROPE_DIM // 2
    partner_hi = pltpu.roll(x, LANES - half, 1)
    partner_lo = pltpu.roll(x, half, 1)
    return x * c_tab + partner_hi * s_lo + partner_lo * s_hi


def _attn_body(q_ref, k_ref, v_ref, g_ref, o_ref, vx_scr, bias_scr, *, seq, tq, hps):
    @pl.when((pl.program_id(0) == 0) & (pl.program_id(1) == 0))
    def _():
        row = lax.broadcasted_iota(jnp.int32, (tq, tq), 0)
        col = lax.broadcasted_iota(jnp.int32, (tq, tq), 1)
        bias_scr[...] = jnp.where(col <= row, 0.0, -jnp.inf).astype(F32)
        vx_scr[:, :, V_HEAD_DIM:] = jnp.ones((hps, seq, V_HEAD_DIM), BF16)

    nt = (((1,), (1,)), ((), ()))
    for hh in range(hps):
        vx_scr[hh, :, :V_HEAD_DIM] = v_ref[:, hh * V_HEAD_DIM:(hh + 1) * V_HEAD_DIM]
    for hh in range(hps):
        gain = g_ref[hh]
        qk_lanes = slice(hh * QK_PAD_DIM, (hh + 1) * QK_PAD_DIM)
        for qi in range(seq // tq):
            q0, q1 = qi * tq, (qi + 1) * tq
            q = q_ref[q0:q1, qk_lanes]
            s = lax.dot_general(q, k_ref[q0:q1, qk_lanes], nt, preferred_element_type=F32) + bias_scr[...]
            m = jnp.max(s, axis=-1, keepdims=True)
            acc = jnp.dot(jnp.exp2(s - m).astype(BF16), vx_scr[hh, q0:q1, :], preferred_element_type=F32)
            for kj in range(qi):
                k0, k1 = kj * tq, (kj + 1) * tq
                s = lax.dot_general(q, k_ref[k0:k1, qk_lanes], nt, preferred_element_type=F32)
                m_new = jnp.maximum(m, jnp.max(s, axis=-1, keepdims=True))
                acc = acc * jnp.exp2(m - m_new) + jnp.dot(
                    jnp.exp2(s - m_new).astype(BF16), vx_scr[hh, k0:k1, :], preferred_element_type=F32)
                m = m_new
            o = acc[:, :V_HEAD_DIM] / acc[:, V_HEAD_DIM:]
            o_ref[q0:q1, hh * V_HEAD_DIM:(hh + 1) * V_HEAD_DIM] = _rms(o, gain).astype(BF16)


def _attention(q, k, v, gains, *, layer, batch, seq, heads, tq=512, hps=4):
    tq = min(tq, seq)
    n_groups = gains.shape[1]
    assert seq % tq == 0 and heads % hps == 0 and n_groups % hps == 0
    gain_block0 = layer * (n_groups // hps)
    q3 = q.reshape(batch, seq, heads * QK_PAD_DIM)
    k3 = k.reshape(batch, seq, heads * QK_PAD_DIM)
    v3 = v.reshape(batch, seq, heads * V_HEAD_DIM)
    est = hps * (2 * (2 * seq * QK_PAD_DIM * 2 + 2 * seq * V_HEAD_DIM * 2) + seq * 2 * V_HEAD_DIM * 2
                 + 6 * tq * tq * 4) + tq * tq * 4
    return pl.pallas_call(
        functools.partial(_attn_body, seq=seq, tq=tq, hps=hps),
        grid=(batch, heads // hps),
        in_specs=[
            pl.BlockSpec((None, seq, hps * QK_PAD_DIM), lambda b, h: (b, 0, h)),
            pl.BlockSpec((None, seq, hps * QK_PAD_DIM), lambda b, h: (b, 0, h)),
            pl.BlockSpec((None, seq, hps * V_HEAD_DIM), lambda b, h: (b, 0, h)),
            pl.BlockSpec((hps, 1, GROUP_DIM), lambda b, h: (gain_block0 + h, 0, 0)),
        ],
        out_specs=pl.BlockSpec((None, seq, hps * V_HEAD_DIM), lambda b, h: (b, 0, h)),
        out_shape=jax.ShapeDtypeStruct((batch, seq, heads * V_HEAD_DIM), BF16),
        scratch_shapes=[pltpu.VMEM((hps, seq, 2 * V_HEAD_DIM), BF16), pltpu.VMEM((tq, tq), F32)],
        compiler_params=pltpu.CompilerParams(
            dimension_semantics=("arbitrary", "arbitrary"), vmem_limit_bytes=_vmem_limit(est)),
        name="attention",
    )(q3, k3, v3, gains.reshape(-1, 1, GROUP_DIM))


def _mixer_in_body(x_ref, g_ref, w_ref, cw_ref, cb_ref, lg_ref, lb_ref, sw_ref, og_ref,
                   qg_ref, kvg_ref, wq_ref, wkv_ref, c_ref, slo_ref, shi_ref,
                   y_ref, q_ref, k_ref, v_ref, u_scr, m_scr, y_scr, wc_scr,
                   *, tm, rows, width, tiles_per_seq, nm, heads):
    groups = width // LANES
    conv0 = nm - LANES + QK_ROPE_DIM

    @pl.when(pl.program_id(0) == 0)
    def _():
        wc_scr[...] = w_ref[:, conv0:conv0 + 5 * width]

    @pl.when(pl.program_id(0) % tiles_per_seq == 0)
    def _():
        u_scr[0:CONF_PAD_ROWS, :] = jnp.zeros((CONF_PAD_ROWS, width), F32)
        m_scr[0:SC_PAD_ROWS, :] = jnp.zeros((SC_PAD_ROWS, width), F32)

    n = _rms(x_ref[...], g_ref[...]).astype(BF16)
    zc = jnp.dot(n, wc_scr[...], preferred_element_type=F32)
    u_scr[CONF_PAD_ROWS:CONF_PAD_ROWS + tm, :] = zc[:, 0:width] * jax.nn.sigmoid(zc[:, width:2 * width])
    sb = zc[:, 2 * width:3 * width]
    m_scr[SC_PAD_ROWS:SC_PAD_ROWS + tm, :] = zc[:, 3 * width:4 * width] * zc[:, 4 * width:5 * width]

    def causal_taps(src_ref, w_ref, t0, lanes, pad_rows, taps, acc):
        win_rows = rows + pad_rows
        win = src_ref[t0:t0 + win_rows, lanes]
        for r in range(SUBLANES):
            shifted = None
            for j in range(taps):
                off = pad_rows - (taps - 1) + j
                if off % SUBLANES != r:
                    continue
                if shifted is None:
                    shifted = win if r == 0 else pltpu.roll(win, win_rows - r, 0)
                base = off - r
                acc = acc + w_ref[j:j + 1, lanes] * shifted[base:base + rows]
        return acc

    for c in range(tm // rows):
        t0 = c * rows
        for cg in range(groups):
            lanes = slice(cg * LANES, (cg + 1) * LANES)
            acc = jnp.broadcast_to(cb_ref[:, lanes], (rows, LANES))
            y_scr[:, lanes] = causal_taps(u_scr, cw_ref, t0, lanes, CONF_PAD_ROWS, CONF_KERNEL, acc)
        y = y_scr[...]
        mu = jnp.mean(y, axis=-1, keepdims=True)
        dlt = y - mu
        var = jnp.mean(dlt * dlt, axis=-1, keepdims=True)
        ln = dlt * lax.rsqrt(var + LN_EPS) * lg_ref[...] + lb_ref[...]
        conf = ln * jax.nn.sigmoid(ln)
        for cg in range(groups):
            lanes = slice(cg * LANES, (cg + 1) * LANES)
            y_ref[t0:t0 + rows, lanes] = _rms(conf[:, lanes], og_ref[cg:cg + 1, :]).astype(BF16)
        for cg in range(groups):
            lanes = slice(cg * LANES, (cg + 1) * LANES)
            acc = causal_taps(m_scr, sw_ref, t0, lanes, SC_PAD_ROWS, SC_KERNEL,
                              jnp.zeros((rows, LANES), F32))
            sc = sb[t0:t0 + rows, lanes] * acc
            y_ref[t0:t0 + rows, width + cg * LANES:width + (cg + 1) * LANES] = _rms(
                sc, og_ref[groups + cg:groups + cg + 1, :]).astype(BF16)

    u_scr[0:CONF_PAD_ROWS, :] = u_scr[tm:tm + CONF_PAD_ROWS, :]
    m_scr[0:SC_PAD_ROWS, :] = m_scr[tm:tm + SC_PAD_ROWS, :]

    zm = jnp.dot(n, w_ref[:, 0:nm], preferred_element_type=F32)
    rank = (nm - LANES) // 2
    c_tab, s_lo, s_hi = c_ref[...], slo_ref[...], shi_ref[...]
    scale = float(QK_NOPE_DIM + QK_ROPE_DIM) ** -0.5 * LOG2_E
    nq = _rms(zm[:, 0:rank], qg_ref[...] * scale).astype(BF16)
    q = jnp.dot(nq, wq_ref[...], preferred_element_type=F32)
    nkv = _rms(zm[:, rank:2 * rank], kvg_ref[...]).astype(BF16)
    kv = jnp.dot(nkv, wkv_ref[...], preferred_element_type=F32)
    k_rope = _rope128(zm[:, 2 * rank:nm], c_tab, s_lo, s_hi).astype(BF16)
    for hd in range(heads):
        base = hd * QK_PAD_DIM
        q_ref[:, base:base + LANES] = q[:, base:base + LANES].astype(BF16)
        q_ref[:, base + LANES:base + 2 * LANES] = _rope128(
            q[:, base + LANES:base + 2 * LANES], c_tab, s_lo, s_hi).astype(BF16)
        k_ref[:, base:base + LANES] = kv[:, base:base + LANES].astype(BF16)
        k_ref[:, base + LANES:base + 2 * LANES] = k_rope
        v_ref[:, hd * LANES:(hd + 1) * LANES] = kv[:, base + LANES:base + 2 * LANES].astype(BF16)


def _mixer_in(h, gains, w_in_b, conf_w, conf_b, ln_g, ln_b, sc_w, out_gains, q_gains, kv_gains, wq, wkv,
              c_tab, s_lo, s_hi, *, layer, seq, width, heads, tm=512, rows=128):
    m, d = h.shape
    depth = conf_w.shape[0]
    nw = w_in_b.shape[2]
    nc = 5 * width
    nm = nw - nc - QK_ROPE_DIM + LANES
    rank = (nm - LANES) // 2
    tm = min(tm, seq)
    rows = min(rows, tm)
    groups = width // LANES
    hq, hv = heads * QK_PAD_DIM, heads * V_HEAD_DIM
    s_blocks = seq // tm
    assert m % tm == 0 and seq % tm == 0 and tm % rows == 0 and nm % LANES == 0
    assert conf_w.shape[1:] == (CONF_KERNEL, width) and sc_w.shape[1:] == (SC_KERNEL, width)
    assert out_gains.shape[1] % (2 * groups) == 0 and wq.shape[1:] == (rank, hq) and wkv.shape[1:] == (rank, hq)
    gain_block = out_gains.shape[1] // (2 * groups) - 1

    def full(shape):
        return pl.BlockSpec((None,) + shape, lambda i: (layer,) + (0,) * len(shape))

    tab = pl.BlockSpec((tm, LANES), lambda i: (i % s_blocks, 0))
    est = (2 * tm * d * 4 + d * nw * 2 + d * nc * 2 + 2 * rank * hq * 2 + 2 * tm * (2 * width + 2 * hq + hv) * 2
           + tm * (nc + nm + 2 * hq) * 4 + tm * d * 2 + 2 * 3 * tm * LANES * 4
           + (2 * tm + CONF_PAD_ROWS + SC_PAD_ROWS + rows) * width * 4)
    return pl.pallas_call(
        functools.partial(_mixer_in_body, tm=tm, rows=rows, width=width, tiles_per_seq=seq // tm, nm=nm,
                          heads=heads),
        grid=(m // tm,),
        in_specs=[
            pl.BlockSpec((tm, d), lambda i: (i, 0)),
            full((1, d)), full((d, nw)),
            full((CONF_KERNEL, width)), full((1, width)), full((1, width)), full((1, width)),
            full((SC_KERNEL, width)),
            pl.BlockSpec((None, 2 * groups, GROUP_DIM), lambda i: (layer, gain_block, 0)),
            full((1, rank)), full((1, rank)), full((rank, hq)), full((rank, hq)), tab, tab, tab,
        ],
        out_specs=[pl.BlockSpec((tm, 2 * width), lambda i: (i, 0)), pl.BlockSpec((tm, hq), lambda i: (i, 0)),
                   pl.BlockSpec((tm, hq), lambda i: (i, 0)), pl.BlockSpec((tm, hv), lambda i: (i, 0))],
        out_shape=[jax.ShapeDtypeStruct((m, 2 * width), BF16), jax.ShapeDtypeStruct((m, hq), BF16),
                   jax.ShapeDtypeStruct((m, hq), BF16), jax.ShapeDtypeStruct((m, hv), BF16)],
        scratch_shapes=[pltpu.VMEM((CONF_PAD_ROWS + tm, width), F32),
                        pltpu.VMEM((SC_PAD_ROWS + tm, width), F32),
                        pltpu.VMEM((rows, width), F32),
                        pltpu.VMEM((d, nc), BF16)],
        compiler_params=pltpu.CompilerParams(
            dimension_semantics=("arbitrary",), vmem_limit_bytes=_vmem_limit(est)),
        name="mixer_in",
    )(h, gains.reshape(depth, 1, d), w_in_b, conf_w, conf_b.reshape(depth, 1, width),
      ln_g.reshape(depth, 1, width), ln_b.reshape(depth, 1, width), sc_w, out_gains,
      q_gains.reshape(depth, 1, rank), kv_gains.reshape(depth, 1, rank), wq, wkv, c_tab, s_lo, s_hi)


def _outproj_body(ya_ref, yc_ref, wa_ref, wc_ref, h_ref, o_ref):
    o_ref[...] = (h_ref[...]
                  + jnp.dot(ya_ref[...], wa_ref[...], preferred_element_type=F32)
                  + jnp.dot(yc_ref[...], wc_ref[...], preferred_element_type=F32))


def _outproj(y_attn, y_conv, w_o, h, *, layer, tm=1024):
    m, d = h.shape
    ka, kc = y_attn.shape[1], y_conv.shape[1]
    tm = min(tm, m)
    assert m % tm == 0 and ka == kc and ka + kc == w_o.shape[1]
    est = 2 * tm * (ka + kc) * 2 + 2 * (ka + kc) * d * 2 + 4 * tm * d * 4 + tm * d * 4
    return pl.pallas_call(
        _outproj_body,
        grid=(m // tm,),
        in_specs=[
            pl.BlockSpec((tm, ka), lambda i: (i, 0)),
            pl.BlockSpec((tm, kc), lambda i: (i, 0)),
            pl.BlockSpec((None, ka, d), lambda i: (layer, 0, 0)),
            pl.BlockSpec((None, kc, d), lambda i: (layer, 1, 0)),
            pl.BlockSpec((tm, d), lambda i: (i, 0)),
        ],
        out_specs=pl.BlockSpec((tm, d), lambda i: (i, 0)),
        out_shape=jax.ShapeDtypeStruct((m, d), F32),
        compiler_params=pltpu.CompilerParams(
            dimension_semantics=("arbitrary",), vmem_limit_bytes=_vmem_limit(est)),
        name="outproj",
    )(y_attn, y_conv, w_o, w_o, h)


def _rope_tables(seq):
    half = QK_ROPE_DIM // 2
    inv = 1.0 / (ROPE_THETA ** (jnp.arange(0, QK_ROPE_DIM, 2, dtype=F32) / QK_ROPE_DIM))
    ang = jnp.arange(seq, dtype=F32)[:, None] * inv[None, :]
    cos, sin = jnp.cos(ang), jnp.sin(ang)
    zero = jnp.zeros((seq, half), F32)
    c_tab = jnp.concatenate([cos, cos, zero, zero], axis=-1)
    s_lo = jnp.concatenate([-sin, zero, zero, zero], axis=-1)
    s_hi = jnp.concatenate([zero, sin, zero, zero], axis=-1)
    return c_tab, s_lo, s_hi


def kernel(x, ffn1_norm, ffn1_w_gate, ffn1_w_up, ffn1_w_down, mix_norm, w_in, q_norm, w_uq, kv_norm, w_ukv, conf_dw_w, conf_dw_b, conf_ln_g, conf_ln_b, sc_w, mix_out_norm, w_o, ffn2_norm, ffn2_w_gate, ffn2_w_up, ffn2_w_down, final_norm):
    batch, seq, d = x.shape
    depth = w_in.shape[0]
    q_rank = q_norm.shape[1]
    kv_rank = kv_norm.shape[1]
    conf_width = conf_dw_w.shape[2]
    sc_width = sc_w.shape[2]
    heads = w_ukv.shape[2] // (QK_NOPE_DIM + V_HEAD_DIM)
    assert w_uq.shape[2] == heads * (QK_NOPE_DIM + QK_ROPE_DIM)
    assert conf_width == sc_width == q_rank == kv_rank
    assert w_in.shape[2] == q_rank + kv_rank + QK_ROPE_DIM + 2 * conf_width + 3 * sc_width
    assert heads * V_HEAD_DIM + conf_width + sc_width == w_o.shape[1]

    c_tab, s_lo, s_hi = _rope_tables(seq)
    h = x.reshape(batch * seq, d)
    w_in_b = w_in.astype(BF16)
    wq = w_uq.reshape(depth, q_rank, heads, QK_NOPE_DIM + QK_ROPE_DIM)
    wq_p = jnp.pad(wq, ((0, 0), (0, 0), (0, 0), (0, QK_PAD_DIM - QK_NOPE_DIM - QK_ROPE_DIM))).reshape(
        depth, q_rank, heads * QK_PAD_DIM).astype(BF16)
    wkv = w_ukv.astype(BF16)
    w_o_b = w_o.astype(BF16)

    for l in range(depth):
        h = _ffn(h, ffn1_norm, ffn1_w_gate, ffn1_w_up, ffn1_w_down, final_norm, layer=l, final_norm=False,
                 in_place=(l > 0))
        y_conv, q, k, v = _mixer_in(h, mix_norm, w_in_b, conf_dw_w, conf_dw_b, conf_ln_g, conf_ln_b, sc_w,
                                    mix_out_norm, q_norm, kv_norm, wq_p, wkv, c_tab, s_lo, s_hi,
                                    layer=l, seq=seq, width=conf_width, heads=heads)
        y_attn = _attention(q, k, v, mix_out_norm, layer=l, batch=batch, seq=seq, heads=heads)
        h = _outproj(y_attn.reshape(batch * seq, -1), y_conv, w_o_b, h, layer=l)
        h = _ffn(h, ffn2_norm, ffn2_w_gate, ffn2_w_up, ffn2_w_down, final_norm, layer=l,
                 final_norm=(l == depth - 1), in_place=True)
    return h.reshape(batch, seq, d)
```
